```python
import math
import jax, jax.numpy as jnp
from jax import lax
import numpy as np

D_MODEL = 1024
BATCH = 8
SEQ = 8192
DEPTH = 1
DEC_BATCH = 16
DEC_SEQ = 32
PAST_LEN = 4096

CHUNK = 64
N_META = 16
CONV_DIM = D_MODEL // 2
CONV_W = 3
N_HEADS = 8
N_KV_HEADS = 2
HEAD_DIM = 64
GROUP = N_HEADS // N_KV_HEADS
Q_DIM = N_HEADS * HEAD_DIM
KV_DIM = N_KV_HEADS * HEAD_DIM
MIX_DIM = CONV_DIM + Q_DIM
IN_DIM = 3 * CONV_DIM + Q_DIM + 2 * KV_DIM
WINDOW = 128
WIN_CHUNKS = WINDOW // CHUNK
N_BUCKETS = 32
MAX_DISTANCE = 128
D_FF = 4 * D_MODEL
EPS = 1e-6
SPLITS = [CONV_DIM, 2 * CONV_DIM, 3 * CONV_DIM, 3 * CONV_DIM + Q_DIM, 3 * CONV_DIM + Q_DIM + KV_DIM]

kernel_name = "hybrid_conv_swa_sink_stream_step"


def rms_norm(x, g):
    xf = x.astype(jnp.float32)
    y = xf * lax.rsqrt(jnp.mean(xf * xf, axis=-1, keepdims=True) + EPS)
    return (y * g.astype(jnp.float32)).astype(x.dtype)


def t5_bucket(rp):
    nb = N_BUCKETS // 2
    max_exact = nb // 2
    ret = jnp.where(rp > 0, nb, 0)
    n = jnp.abs(rp)
    nf = jnp.maximum(n, 1).astype(jnp.float32)
    large = max_exact + (jnp.log(nf / max_exact) / math.log(MAX_DISTANCE / max_exact) * (nb - max_exact)).astype(jnp.int32)
    large = jnp.minimum(large, nb - 1)
    return ret + jnp.where(n < max_exact, n, large)


def rel_bias(table, q_pos, k_pos):
    b = table[t5_bucket(k_pos - q_pos)]
    b = jnp.moveaxis(b, -1, -3)
    return b.reshape(b.shape[:-3] + (N_KV_HEADS, GROUP) + b.shape[-2:]).astype(jnp.float32)


def sink_attention(q, k, v, bias, sinks, mask=None):
    s = jnp.einsum("...qhgd,...khd->...hgqk", q.astype(jnp.float32), k.astype(jnp.float32)) * (HEAD_DIM ** -0.5) + bias
    if mask is not None:
        s = jnp.where(mask, s, -jnp.inf)
    sink = sinks.astype(jnp.float32).reshape(N_KV_HEADS, GROUP, 1, 1)
    m = jnp.maximum(jnp.max(s, axis=-1, keepdims=True), sink)
    p = jnp.exp(s - m)
    denom = jnp.sum(p, axis=-1, keepdims=True) + jnp.exp(sink - m)
    o = jnp.einsum("...hgqk,...khd->...qhgd", p / denom, v.astype(jnp.float32))
    return o.astype(q.dtype)


def causal_conv(up, w, length):
    y = up[:, 0:length] * w[0]
    for j in range(1, CONV_W):
        y = y + up[:, j:j + length] * w[j]
    return y


def in_projection(xn, w_in):
    return jnp.split(xn @ w_in, SPLITS, axis=-1)


def prompt_mixer(xn, w_in, conv_w, sinks, table):
    bsz, L, _ = xn.shape
    b, c, u, q, k, v = in_projection(xn, w_in)
    uc = c * u
    up = jnp.pad(uc, ((0, 0), (CONV_W - 1, 0), (0, 0)))
    y_conv = b * causal_conv(up, conv_w, L)
    conv_state = uc[:, L - (CONV_W - 1):]
    q = q.reshape(bsz, L, N_KV_HEADS, GROUP, HEAD_DIM)
    k = k.reshape(bsz, L, N_KV_HEADS, HEAD_DIM)
    v = v.reshape(bsz, L, N_KV_HEADS, HEAD_DIM)
    km, vm = k[:, :N_META], v[:, :N_META]
    pm = jnp.arange(N_META, dtype=jnp.int32)
    o_meta = sink_attention(q[:, :N_META], km, vm, rel_bias(table, pm[:, None], pm[None, :]), sinks)
    S = L - N_META
    nc = S // CHUNK
    qf = q[:, N_META:].reshape(bsz, nc, CHUNK, N_KV_HEADS, GROUP, HEAD_DIM)

    def band(t):
        tp = jnp.pad(t, ((0, 0), (WIN_CHUNKS * CHUNK, 0), (0, 0), (0, 0)))
        tp = tp.reshape(bsz, nc + WIN_CHUNKS, CHUNK, N_KV_HEADS, HEAD_DIM)
        return jnp.concatenate([tp[:, j:j + nc] for j in range(WIN_CHUNKS + 1)], axis=2)

    def with_meta(tm, t):
        tmb = jnp.broadcast_to(tm[:, None], (bsz, nc) + tm.shape[1:])
        return jnp.concatenate([tmb, band(t)], axis=2)

    kf = with_meta(km, k[:, N_META:])
    vf = with_meta(vm, v[:, N_META:])
    ci = jnp.arange(nc, dtype=jnp.int32)[:, None, None]
    qi = jnp.arange(CHUNK, dtype=jnp.int32)[None, :, None]
    r = jnp.arange((WIN_CHUNKS + 1) * CHUNK, dtype=jnp.int32)[None, None, :]
    frame_k = ci * CHUNK - WIN_CHUNKS * CHUNK + r
    q_pos = N_META + ci * CHUNK + qi
    k_pos = jnp.concatenate([jnp.broadcast_to(pm[None, None, :], (nc, 1, N_META)), N_META + frame_k], axis=-1)
    valid = jnp.concatenate([jnp.ones((nc, 1, N_META), dtype=bool), frame_k >= 0], axis=-1)
    bias = rel_bias(table, q_pos, k_pos)
    o_f = sink_attention(qf, kf, vf, bias, sinks, valid[:, None, None])
    y_attn = jnp.concatenate([o_meta.reshape(bsz, N_META, Q_DIM), o_f.reshape(bsz, S, Q_DIM)], axis=1)
    n_keep = min(WINDOW, S)
    return y_conv, y_attn, k[:, L - n_keep:], v[:, L - n_keep:], km, vm, conv_state


def sample_mixer(xn, w_in, conv_w, sinks, table, ck, cv, cmk, cmv, conv_state):
    bsz, S, _ = xn.shape
    b, c, u, q, k, v = in_projection(xn, w_in)
    uc = c * u
    up = jnp.concatenate([conv_state.astype(uc.dtype), uc], axis=1)
    y_conv = b * causal_conv(up, conv_w, S)
    new_conv = up[:, S:]
    q = q.reshape(bsz, S, N_KV_HEADS, GROUP, HEAD_DIM)
    k = k.reshape(bsz, S, N_KV_HEADS, HEAD_DIM)
    v = v.reshape(bsz, S, N_KV_HEADS, HEAD_DIM)
    n_win = ck.shape[1]
    k_all = jnp.concatenate([cmk.astype(k.dtype), ck.astype(k.dtype), k], axis=1)
    v_all = jnp.concatenate([cmv.astype(v.dtype), cv.astype(v.dtype), v], axis=1)
    q_pos = N_META + PAST_LEN + jnp.arange(S, dtype=jnp.int32)[:, None]
    k_pos = jnp.concatenate([
        jnp.arange(N_META, dtype=jnp.int32),
        N_META + PAST_LEN - n_win + jnp.arange(n_win, dtype=jnp.int32),
        N_META + PAST_LEN + jnp.arange(S, dtype=jnp.int32)])[None, :]
    o = sink_attention(q, k_all, v_all, rel_bias(table, q_pos, k_pos), sinks)
    return y_conv, o.reshape(bsz, S, Q_DIM), k, v, new_conv


def merge(y_conv, y_attn, g_conv, g_attn, w_out):
    return jnp.concatenate([rms_norm(y_conv, g_conv), rms_norm(y_attn, g_attn)], axis=-1) @ w_out


def sq_relu_mlp(x, w_up, w_down):
    h = jax.nn.relu(x @ w_up)
    return (h * h) @ w_down


def setup_inputs(seed: int = 0) -> dict:
    key = jax.random.key(seed)
    ks = jax.random.split(key, 20)
    n_win = min(WINDOW, PAST_LEN)
    f32 = jnp.float32
    nrm = lambda k, s, sc: jax.random.normal(k, s, f32) * sc
    gain = lambda k, s: 1.0 + 0.02 * jax.random.normal(k, s, f32)
    return {
        "x_prompt": nrm(ks[0], (BATCH, SEQ, D_MODEL), 1.0),
        "x_sample": nrm(ks[1], (DEC_BATCH, DEC_SEQ, D_MODEL), 1.0),
        "cache_k": nrm(ks[2], (DEPTH, DEC_BATCH, n_win, N_KV_HEADS, HEAD_DIM), 1.0),
        "cache_v": nrm(ks[3], (DEPTH, DEC_BATCH, n_win, N_KV_HEADS, HEAD_DIM), 1.0),
        "cache_meta_k": nrm(ks[4], (DEPTH, DEC_BATCH, N_META, N_KV_HEADS, HEAD_DIM), 1.0),
        "cache_meta_v": nrm(ks[5], (DEPTH, DEC_BATCH, N_META, N_KV_HEADS, HEAD_DIM), 1.0),
        "state_conv": nrm(ks[6], (DEPTH, DEC_BATCH, CONV_W - 1, CONV_DIM), 1.0),
        "meta_tokens": nrm(ks[7], (N_META, D_MODEL), 1.0),
        "norm_mix": gain(ks[8], (DEPTH, D_MODEL)),
        "w_in": nrm(ks[9], (DEPTH, D_MODEL, IN_DIM), D_MODEL ** -0.5),
        "conv_w": nrm(ks[10], (DEPTH, CONV_W, CONV_DIM), CONV_W ** -0.5),
        "attn_sinks": nrm(ks[11], (DEPTH, N_HEADS), 0.5),
        "rel_bias_table": nrm(ks[12], (N_BUCKETS, N_HEADS), 0.2),
        "norm_conv_out": gain(ks[13], (DEPTH, CONV_DIM)),
        "norm_attn_out": gain(ks[14], (DEPTH, Q_DIM)),
        "w_out": nrm(ks[15], (DEPTH, MIX_DIM, D_MODEL), MIX_DIM ** -0.5),
        "norm_mlp": gain(ks[16], (DEPTH, D_MODEL)),
        "w_up": nrm(ks[17], (DEPTH, D_MODEL, D_FF), D_MODEL ** -0.5),
        "w_down": nrm(ks[18], (DEPTH, D_FF, D_MODEL), D_FF ** -0.5),
        "norm_final": gain(ks[19], (D_MODEL,)),
    }


def reference(x_prompt, x_sample, cache_k, cache_v, cache_meta_k, cache_meta_v, state_conv, meta_tokens,
              norm_mix, w_in, conv_w, attn_sinks, rel_bias_table, norm_conv_out, norm_attn_out, w_out,
              norm_mlp, w_up, w_down, norm_final):
    bsz = x_prompt.shape[0]
    hp = jnp.concatenate([jnp.broadcast_to(meta_tokens.astype(x_prompt.dtype)[None], (bsz, N_META, D_MODEL)), x_prompt], axis=1)
    hs = x_sample
    pk, pv, pmk, pmv, pc, sk, sv, sc = [], [], [], [], [], [], [], []
    for l in range(DEPTH):
        yc, ya, kw, vw, mk, mv, cs = prompt_mixer(rms_norm(hp, norm_mix[l]), w_in[l], conv_w[l], attn_sinks[l], rel_bias_table)
        hp = hp + merge(yc, ya, norm_conv_out[l], norm_attn_out[l], w_out[l])
        hp = hp + sq_relu_mlp(rms_norm(hp, norm_mlp[l]), w_up[l], w_down[l])
        pk.append(kw); pv.append(vw); pmk.append(mk); pmv.append(mv); pc.append(cs)
        yc, ya, kn, vn, cn = sample_mixer(rms_norm(hs, norm_mix[l]), w_in[l], conv_w[l], attn_sinks[l], rel_bias_table,
                                          cache_k[l], cache_v[l], cache_meta_k[l], cache_meta_v[l], state_conv[l])
        hs = hs + merge(yc, ya, norm_conv_out[l], norm_attn_out[l], w_out[l])
        hs = hs + sq_relu_mlp(rms_norm(hs, norm_mlp[l]), w_up[l], w_down[l])
        sk.append(kn); sv.append(vn); sc.append(cn)
    y_prompt = rms_norm(hp, norm_final)[:, N_META:]
    y_sample = rms_norm(hs, norm_final)
    return (y_prompt, y_sample, jnp.stack(pk), jnp.stack(pv), jnp.stack(pmk), jnp.stack(pmv), jnp.stack(pc),
            jnp.stack(sk), jnp.stack(sv), jnp.stack(sc))
```

```python
import functools
import math

import jax
import jax.numpy as jnp
from jax import lax
from jax.experimental import pallas as pl
from jax.experimental.pallas import tpu as pltpu

D_MODEL = 1024
CHUNK = 64
N_META = 16
CONV_DIM = 512
CONV_W = 3
N_HEADS = 8
N_KV_HEADS = 2
HEAD_DIM = 64
GROUP = N_HEADS // N_KV_HEADS
Q_DIM = N_HEADS * HEAD_DIM
KV_DIM = N_KV_HEADS * HEAD_DIM
IN_DIM = 3 * CONV_DIM + Q_DIM + 2 * KV_DIM
WINDOW = 128
WIN_CHUNKS = WINDOW // CHUNK
BAND = (WIN_CHUNKS + 1) * CHUNK
N_BUCKETS = 32
MAX_DISTANCE = 128
D_FF = 4 * D_MODEL
EPS = 1e-6
PAST_LEN = 4096

OFF_B, OFF_C, OFF_U = 0, CONV_DIM, 2 * CONV_DIM
OFF_Q = 3 * CONV_DIM
OFF_K = OFF_Q + Q_DIM
OFF_V = OFF_K + KV_DIM

LANES = 128
SUBLANES = 8
VMEM_LIMIT_BYTES = 56 * 1024 * 1024
MASK_VALUE = -1e30

MIXER_TILE = 512
MLP_TILE = 512
N_VARIANTS = 2 * N_KV_HEADS

BF16 = jnp.bfloat16
F32 = jnp.float32


def _rms(x, g):
    ms = jnp.mean(x * x, axis=-1, keepdims=True)
    return x * lax.rsqrt(ms + EPS) * g


def _dot(a, b):
    return jnp.dot(a, b, preferred_element_type=F32)


def _dot_nt(a, b):
    return lax.dot_general(a, b, (((1,), (1,)), ((), ())), preferred_element_type=F32)


def _lane_half_variants(kv):
    lane = lax.broadcasted_iota(jnp.int32, kv.shape, 1)
    low = lane < HEAD_DIM
    swapped = pltpu.roll(kv, HEAD_DIM, axis=1)
    zero = jnp.zeros_like(kv)
    out = (jnp.where(low, kv, zero), jnp.where(low, zero, swapped),
           jnp.where(low, swapped, zero), jnp.where(low, zero, kv))
    return tuple(o.astype(BF16) for o in out)


def _attend(q_pairs, keys, values, bias, sink):
    s = _dot_nt(q_pairs, keys) + bias
    m = jnp.maximum(jnp.max(s, axis=-1, keepdims=True), sink)
    p = jnp.exp(s - m)
    denom = jnp.sum(p, axis=-1, keepdims=True) + jnp.exp(sink - m)
    return _dot(p.astype(BF16), values) * (1.0 / denom)


def _aux_kernel(xs_ref, meta_ref, g_mix_ref, w_in_ref, conv_w_ref, state_ref, ck_ref, cv_ref,
                cmk_ref, cmv_ref, bias_ref, sink_ref, g_conv_ref, g_attn_ref, w_out_ref,
                h1_ref, sk_ref, sv_ref, sconv_ref, mkv_ref, mk_ref, mv_ref, muc_ref,
                ucb_ref, q_ref, ya_ref, yc_ref, *, n_batch, seq):
    g_mix = g_mix_ref[...]
    w_in = w_in_ref[...]

    mproj = _dot(_rms(meta_ref[...], g_mix).astype(BF16), w_in)
    mk = mproj[:, OFF_K:OFF_K + KV_DIM]
    mv = mproj[:, OFF_V:OFF_V + KV_DIM]
    mk_ref[...] = mk
    mv_ref[...] = mv
    muc_ref[...] = mproj[:, OFF_C:OFF_C + CONV_DIM] * mproj[:, OFF_U:OFF_U + CONV_DIM]
    for i, var in enumerate(_lane_half_variants(mk) + _lane_half_variants(mv)):
        mkv_ref[i] = var

    xs = xs_ref[...]
    proj = _dot(_rms(xs, g_mix).astype(BF16), w_in)
    uc = proj[:, OFF_C:OFF_C + CONV_DIM] * proj[:, OFF_U:OFF_U + CONV_DIM]
    sk_ref[...] = proj[:, OFF_K:OFF_K + KV_DIM]
    sv_ref[...] = proj[:, OFF_V:OFF_V + KV_DIM]
    q_ref[...] = (proj[:, OFF_Q:OFF_Q + Q_DIM] * (HEAD_DIM ** -0.5)).astype(BF16)
    yc_ref[...] = proj[:, OFF_B:OFF_B + CONV_DIM]
    w0 = conv_w_ref[0:1, :]
    w1 = conv_w_ref[1:2, :]
    w2 = conv_w_ref[2:3, :]
    head = SUBLANES
    for b in range(n_batch):
        ucb_ref[b, head - 2:head, :] = state_ref[b]
        ucb_ref[b, head:head + seq, :] = uc[b * seq:(b + 1) * seq]
        sconv_ref[b] = uc[(b + 1) * seq - 2:(b + 1) * seq]

    def batch_body(b, carry):
        r0 = pl.multiple_of(b * seq, seq)
        rows = pl.ds(r0, seq)
        conv = (w0 * ucb_ref[b, head - 2:head - 2 + seq, :] + w1 * ucb_ref[b, head - 1:head - 1 + seq, :]
                + w2 * ucb_ref[b, head:head + seq, :])
        yc_ref[rows, :] = yc_ref[rows, :] * conv
        k_all = jnp.concatenate([cmk_ref[b], ck_ref[b], sk_ref[rows, :]], axis=0)
        v_all = jnp.concatenate([cmv_ref[b], cv_ref[b], sv_ref[rows, :]], axis=0)
        k_var = _lane_half_variants(k_all)
        v_var = _lane_half_variants(v_all)
        for j in range(N_KV_HEADS):
            c0 = j * GROUP * HEAD_DIM
            q_pairs = jnp.concatenate([q_ref[rows, c0:c0 + LANES], q_ref[rows, c0 + LANES:c0 + 2 * LANES]],
                                      axis=0)
            o = None
            for half in range(2):
                i = 2 * j + half
                oh = _attend(q_pairs, k_var[i], v_var[i], bias_ref[i], sink_ref[i])
                o = oh if o is None else o + oh
            ya_ref[rows, c0:c0 + LANES] = o[0:seq]
            ya_ref[rows, c0 + LANES:c0 + 2 * LANES] = o[seq:2 * seq]
        return carry

    lax.fori_loop(0, n_batch, batch_body, 0)

    yc_n = _rms(yc_ref[...], g_conv_ref[...]).astype(BF16)
    ya_n = _rms(ya_ref[...], g_attn_ref[...]).astype(BF16)
    h1_ref[...] = xs + _dot(yc_n, w_out_ref[0:CONV_DIM, :]) + _dot(ya_n, w_out_ref[CONV_DIM:, :])


def _aux_call(xs, meta, g_mix, w_in, conv_w, state, ck, cv, cmk, cmv, bias, sink, g_conv, g_attn, w_out,
              *, n_batch, seq):
    n_tok = n_batch * seq
    out_shape = (
        jax.ShapeDtypeStruct((n_tok, D_MODEL), F32),
        jax.ShapeDtypeStruct((n_tok, KV_DIM), F32),
        jax.ShapeDtypeStruct((n_tok, KV_DIM), F32),
        jax.ShapeDtypeStruct((n_batch, CONV_W - 1, CONV_DIM), F32),
        jax.ShapeDtypeStruct((2 * N_VARIANTS, N_META, LANES), BF16),
        jax.ShapeDtypeStruct((N_META, KV_DIM), F32),
        jax.ShapeDtypeStruct((N_META, KV_DIM), F32),
        jax.ShapeDtypeStruct((N_META, CONV_DIM), F32),
    )
    return pl.pallas_call(
        functools.partial(_aux_kernel, n_batch=n_batch, seq=seq),
        out_shape=out_shape,
        scratch_shapes=[
            pltpu.VMEM((n_batch, SUBLANES + seq, CONV_DIM), F32),
            pltpu.VMEM((n_tok, Q_DIM), BF16),
            pltpu.VMEM((n_tok, Q_DIM), F32),
            pltpu.VMEM((n_tok, CONV_DIM), F32),
        ],
        compiler_params=pltpu.CompilerParams(vmem_limit_bytes=VMEM_LIMIT_BYTES),
        name="aux_sample_mixer",
    )(xs, meta, g_mix, w_in, conv_w, state, ck, cv, cmk, cmv, bias, sink, g_conv, g_attn, w_out)


def _mixer_kernel(x_ref, g_mix_ref, w_in_ref, conv_w_ref, mkv_ref, muc_ref, bias_ref, sink_ref,
                  g_conv_ref, g_attn_ref, w_out_ref,
                  h1_ref, pk_ref, pv_ref, pc_ref,
                  ucb_ref, kvb_ref, q_ref, ya_ref, *, tile, n_tiles):
    t = pl.program_id(1)
    head = SUBLANES
    chunks = tile // CHUNK

    x = x_ref[...]
    proj = _dot(_rms(x, g_mix_ref[...]).astype(BF16), w_in_ref[...])
    uc = proj[:, OFF_C:OFF_C + CONV_DIM] * proj[:, OFF_U:OFF_U + CONV_DIM]
    k = proj[:, OFF_K:OFF_K + KV_DIM]
    v = proj[:, OFF_V:OFF_V + KV_DIM]

    @pl.when(t == 0)
    def _():
        ucb_ref[head - 2:head, :] = muc_ref[N_META - 2:N_META, :]
        kvb_ref[:, 0:WINDOW, :] = jnp.zeros((2 * N_VARIANTS, WINDOW, LANES), BF16)

    @pl.when(t > 0)
    def _():
        ucb_ref[head - 2:head, :] = ucb_ref[head + tile - 2:head + tile, :]
        kvb_ref[:, 0:WINDOW, :] = kvb_ref[:, tile:tile + WINDOW, :]

    ucb_ref[head:head + tile, :] = uc
    for i, var in enumerate(_lane_half_variants(k) + _lane_half_variants(v)):
        kvb_ref[i, WINDOW:WINDOW + tile, :] = var
    q_ref[...] = (proj[:, OFF_Q:OFF_Q + Q_DIM] * (HEAD_DIM ** -0.5)).astype(BF16)

    conv = (conv_w_ref[0:1, :] * ucb_ref[head - 2:head - 2 + tile, :]
            + conv_w_ref[1:2, :] * ucb_ref[head - 1:head - 1 + tile, :]
            + conv_w_ref[2:3, :] * uc)
    yc = proj[:, OFF_B:OFF_B + CONV_DIM] * conv

    def chunk_body(i, carry):
        cls = jnp.minimum(t * chunks + i, WIN_CHUNKS)
        r0 = pl.multiple_of(i * CHUNK, CHUNK)
        rows = pl.ds(r0, CHUNK)
        band = pl.ds(r0, BAND)
        for j in range(N_KV_HEADS):
            c0 = j * GROUP * HEAD_DIM
            q_pairs = jnp.concatenate([q_ref[rows, c0:c0 + LANES], q_ref[rows, c0 + LANES:c0 + 2 * LANES]],
                                      axis=0)
            o = None
            for half in range(2):
                iv = 2 * j + half
                keys = jnp.concatenate([kvb_ref[iv, band, :], mkv_ref[iv]], axis=0)
                vals = jnp.concatenate([kvb_ref[N_VARIANTS + iv, band, :], mkv_ref[N_VARIANTS + iv]], axis=0)
                oh = _attend(q_pairs, keys, vals, bias_ref[cls, iv], sink_ref[iv])
                o = oh if o is None else o + oh
            ya_ref[rows, c0:c0 + LANES] = o[0:CHUNK]
            ya_ref[rows, c0 + LANES:c0 + 2 * LANES] = o[CHUNK:2 * CHUNK]
        return carry

    lax.fori_loop(0, chunks, chunk_body, 0)

    yc_n = _rms(yc, g_conv_ref[...]).astype(BF16)
    ya_n = _rms(ya_ref[...], g_attn_ref[...]).astype(BF16)
    h1_ref[...] = x + _dot(yc_n, w_out_ref[0:CONV_DIM, :]) + _dot(ya_n, w_out_ref[CONV_DIM:, :])

    @pl.when(t == n_tiles - 1)
    def _():
        pk_ref[...] = k[tile - WINDOW:tile]
        pv_ref[...] = v[tile - WINDOW:tile]
        pc_ref[...] = uc[tile - (CONV_W - 1):tile]


def _resident(shape):
    return pl.BlockSpec(shape, lambda *_: (0,) * len(shape), pipeline_mode=pl.Buffered(1))


def _mixer_call(x, g_mix, w_in, conv_w, mkv, muc, bias, sink, g_conv, g_attn, w_out):
    n_batch, seq, _ = x.shape
    tile = MIXER_TILE
    n_tiles = seq // tile
    assert seq % tile == 0 and tile % CHUNK == 0 and tile >= WINDOW
    out_shape = (
        jax.ShapeDtypeStruct((n_batch, seq, D_MODEL), F32),
        jax.ShapeDtypeStruct((n_batch, WINDOW, KV_DIM), F32),
        jax.ShapeDtypeStruct((n_batch, WINDOW, KV_DIM), F32),
        jax.ShapeDtypeStruct((n_batch, CONV_W - 1, CONV_DIM), F32),
    )
    per_batch = lambda shape: pl.BlockSpec((None,) + shape, lambda b, t: (b, 0, 0))
    return pl.pallas_call(
        functools.partial(_mixer_kernel, tile=tile, n_tiles=n_tiles),
        grid=(n_batch, n_tiles),
        in_specs=[
            pl.BlockSpec((None, tile, D_MODEL), lambda b, t: (b, t, 0)),
            _resident(g_mix.shape), _resident(w_in.shape), _resident(conv_w.shape),
            _resident(mkv.shape), _resident(muc.shape), _resident(bias.shape), _resident(sink.shape),
            _resident(g_conv.shape), _resident(g_attn.shape), _resident(w_out.shape),
        ],
        out_specs=(
            pl.BlockSpec((None, tile, D_MODEL), lambda b, t: (b, t, 0)),
            per_batch((WINDOW, KV_DIM)), per_batch((WINDOW, KV_DIM)), per_batch((CONV_W - 1, CONV_DIM)),
        ),
        out_shape=out_shape,
        scratch_shapes=[
            pltpu.VMEM((SUBLANES + tile, CONV_DIM), F32),
            pltpu.VMEM((2 * N_VARIANTS, WINDOW + tile, LANES), BF16),
            pltpu.VMEM((tile, Q_DIM), BF16),
            pltpu.VMEM((tile, Q_DIM), F32),
        ],
        compiler_params=pltpu.CompilerParams(
            dimension_semantics=("arbitrary", "arbitrary"), vmem_limit_bytes=VMEM_LIMIT_BYTES),
        name="prompt_mixer",
    )(x, g_mix, w_in, conv_w, mkv, muc, bias, sink, g_conv, g_attn, w_out)


def _mlp_kernel(h_ref, g_mlp_ref, w_up_ref, w_down_ref, g_final_ref, y_ref):
    h = h_ref[...]
    u = jnp.maximum(_dot(_rms(h, g_mlp_ref[...]).astype(BF16), w_up_ref[...]), 0.0)
    h2 = h + _dot((u * u).astype(BF16), w_down_ref[...])
    y_ref[...] = _rms(h2, g_final_ref[...])


def _mlp_call(h, g_mlp, w_up, w_down, g_final):
    n_tok = h.shape[0]
    tile = min(MLP_TILE, n_tok)
    assert n_tok % tile == 0
    return pl.pallas_call(
        _mlp_kernel,
        grid=(n_tok // tile,),
        in_specs=[
            pl.BlockSpec((tile, D_MODEL), lambda i: (i, 0)),
            _resident(g_mlp.shape), _resident(w_up.shape), _resident(w_down.shape), _resident(g_final.shape),
        ],
        out_specs=pl.BlockSpec((tile, D_MODEL), lambda i: (i, 0)),
        out_shape=jax.ShapeDtypeStruct((n_tok, D_MODEL), F32),
        compiler_params=pltpu.CompilerParams(
            dimension_semantics=("arbitrary",), vmem_limit_bytes=VMEM_LIMIT_BYTES),
        name="mlp",
    )(h, g_mlp, w_up, w_down, g_final)


def _t5_bucket(rp):
    nb = N_BUCKETS // 2
    max_exact = nb // 2
    ret = jnp.where(rp > 0, nb, 0)
    n = jnp.abs(rp)
    nf = jnp.maximum(n, 1).astype(F32)
    large = max_exact + (jnp.log(nf / max_exact) / math.log(MAX_DISTANCE / max_exact)
                         * (nb - max_exact)).astype(jnp.int32)
    large = jnp.minimum(large, nb - 1)
    return ret + jnp.where(n < max_exact, n, large)


def _pair_rows(per_head):
    out = []
    for j in range(N_KV_HEADS):
        for half in range(2):
            out.append(jnp.concatenate([per_head[GROUP * j + half], per_head[GROUP * j + 2 + half]], axis=0))
    return jnp.stack(out)


def _head_bias(table, q_pos, k_pos, valid=None):
    b = table[_t5_bucket(k_pos[None, :] - q_pos[:, None])].astype(F32)
    if valid is not None:
        b = jnp.where(valid[None, :, None], b, MASK_VALUE)
    return _pair_rows(jnp.moveaxis(b, -1, 0))


def _prompt_bias(table):
    qi = jnp.arange(CHUNK, dtype=jnp.int32)
    r = jnp.arange(BAND, dtype=jnp.int32)
    pm = jnp.arange(N_META, dtype=jnp.int32)
    out = []
    for c in range(WIN_CHUNKS + 1):
        frame_k = c * CHUNK - WIN_CHUNKS * CHUNK + r
        k_pos = jnp.concatenate([N_META + frame_k, pm])
        valid = jnp.concatenate([frame_k >= 0, jnp.ones((N_META,), bool)])
        out.append(_head_bias(table, N_META + c * CHUNK + qi, k_pos, valid))
    return jnp.stack(out)


def _sample_bias(table, n_win, seq):
    s = jnp.arange(seq, dtype=jnp.int32)
    k_pos = jnp.concatenate([jnp.arange(N_META, dtype=jnp.int32),
                             N_META + PAST_LEN - n_win + jnp.arange(n_win, dtype=jnp.int32),
                             N_META + PAST_LEN + s])
    return _head_bias(table, N_META + PAST_LEN + s, k_pos)


def _sink_columns(sinks, rows):
    per_head = jnp.broadcast_to(sinks.astype(F32)[:, None, None], (N_HEADS, rows, 1))
    return _pair_rows(per_head)


def kernel(x_prompt, x_sample, cache_k, cache_v, cache_meta_k, cache_meta_v, state_conv, meta_tokens,
           norm_mix, w_in, conv_w, attn_sinks, rel_bias_table, norm_conv_out, norm_attn_out, w_out,
           norm_mlp, w_up, w_down, norm_final):
    n_batch, seq, _ = x_prompt.shape
    s_batch, s_seq, _ = x_sample.shape
    n_win = cache_k.shape[2]
    row = lambda a: a.reshape(1, -1)

    w_in_b = w_in[0].astype(BF16)
    w_out_b = w_out[0].astype(BF16)
    w_up_b = w_up[0].astype(BF16)
    w_down_b = w_down[0].astype(BF16)
    g_mix, g_conv, g_attn = row(norm_mix[0]), row(norm_conv_out[0]), row(norm_attn_out[0])
    g_mlp, g_final = row(norm_mlp[0]), row(norm_final)

    h1_s, s_k, s_v, s_conv, mkv, mk, mv, muc = _aux_call(
        x_sample.reshape(s_batch * s_seq, D_MODEL), meta_tokens, g_mix, w_in_b, conv_w[0], state_conv[0],
        cache_k[0].reshape(s_batch, n_win, KV_DIM), cache_v[0].reshape(s_batch, n_win, KV_DIM),
        cache_meta_k[0].reshape(s_batch, N_META, KV_DIM), cache_meta_v[0].reshape(s_batch, N_META, KV_DIM),
        _sample_bias(rel_bias_table, n_win, s_seq), _sink_columns(attn_sinks[0], s_seq),
        g_conv, g_attn, w_out_b, n_batch=s_batch, seq=s_seq)

    h1_p, p_k, p_v, p_conv = _mixer_call(
        x_prompt, g_mix, w_in_b, conv_w[0], mkv, muc, _prompt_bias(rel_bias_table),
        _sink_columns(attn_sinks[0], CHUNK), g_conv, g_attn, w_out_b)

    y_prompt = _mlp_call(h1_p.reshape(n_batch * seq, D_MODEL), g_mlp, w_up_b, w_down_b, g_final)
    y_sample = _mlp_call(h1_s, g_mlp, w_up_b, w_down_b, g_final)

    kv_shape = lambda a, n, length: a.reshape(1, n, length, N_KV_HEADS, HEAD_DIM)
    meta_shape = (1, n_batch, N_META, N_KV_HEADS, HEAD_DIM)
    return (
        y_prompt.reshape(n_batch, seq, D_MODEL),
        y_sample.reshape(s_batch, s_seq, D_MODEL),
        kv_shape(p_k, n_batch, WINDOW), kv_shape(p_v, n_batch, WINDOW),
        jnp.broadcast_to(mk.reshape(1, 1, N_META, N_KV_HEADS, HEAD_DIM), meta_shape),
        jnp.broadcast_to(mv.reshape(1, 1, N_META, N_KV_HEADS, HEAD_DIM), meta_shape),
        p_conv[None],
        kv_shape(s_k, s_batch, s_seq), kv_shape(s_v, s_batch, s_seq),
        s_conv[None],
    )
```

```python
import functools
import math

import jax
import jax.numpy as jnp
from jax import lax
from jax.experimental import pallas as pl
from jax.experimental.pallas import tpu as pltpu

D_MODEL = 1024
CHUNK = 64
N_META = 16
CONV_DIM = 512
CONV_W = 3
N_HEADS = 8
N_KV_HEADS = 2
HEAD_DIM = 64
GROUP = N_HEADS // N_KV_HEADS
Q_DIM = N_HEADS * HEAD_DIM
KV_DIM = N_KV_HEADS * HEAD_DIM
IN_DIM = 3 * CONV_DIM + Q_DIM + 2 * KV_DIM
WINDOW = 128
WIN_CHUNKS = WINDOW // CHUNK
BAND = (WIN_CHUNKS + 1) * CHUNK
N_BUCKETS = 32
MAX_DISTANCE = 128
D_FF = 4 * D_MODEL
EPS = 1e-6
PAST_LEN = 4096

OFF_B, OFF_C, OFF_U = 0, CONV_DIM, 2 * CONV_DIM
OFF_Q = 3 * CONV_DIM
OFF_K = OFF_Q + Q_DIM
OFF_V = OFF_K + KV_DIM

LANES = 128
SUBLANES = 8
BF16_ROWS = 16
VMEM_LIMIT_BYTES = 56 * 1024 * 1024
MASK_VALUE = -1e30

MIXER_TILE = 512
MLP_TILE = 512
N_VARIANTS = 2 * N_KV_HEADS
TAIL_ROWS = BF16_ROWS
META_ROWS = N_META + TAIL_ROWS
MASKED_ID = -1
SINK_ID = -2

BF16 = jnp.bfloat16
F32 = jnp.float32


def _rms(x, g):
    ms = jnp.mean(x * x, axis=-1, keepdims=True)
    return x * lax.rsqrt(ms + EPS) * g


def _dot(a, b):
    return jnp.dot(a, b, preferred_element_type=F32)


def _dot_nt(a, b):
    return lax.dot_general(a, b, (((1,), (1,)), ((), ())), preferred_element_type=F32)


def _lane_half_variants(kv):
    lane = lax.broadcasted_iota(jnp.int32, kv.shape, 1)
    low = lane < HEAD_DIM
    swapped = pltpu.roll(kv, HEAD_DIM, axis=1)
    zero = jnp.zeros_like(kv)
    out = (jnp.where(low, kv, zero), jnp.where(low, zero, swapped),
           jnp.where(low, swapped, zero), jnp.where(low, zero, kv))
    return tuple(o.astype(BF16) for o in out)


def _denominator_lanes(n_rows, n_counted, half):
    row = lax.broadcasted_iota(jnp.int32, (n_rows, LANES), 0)
    lane = lax.broadcasted_iota(jnp.int32, (n_rows, LANES), 1)
    in_half = (lane < HEAD_DIM) if half == 0 else (lane >= HEAD_DIM)
    return jnp.where(in_half & (row < n_counted), 1.0, 0.0).astype(BF16)


def _build_bias(ids, table_ref, sinks_ref):
    out = []
    for h in range(N_HEADS):
        acc = jnp.full(ids.shape, MASK_VALUE, F32)
        for bucket in range(N_BUCKETS):
            acc = jnp.where(ids == bucket, table_ref[bucket, h], acc)
        out.append(jnp.where(ids == SINK_ID, sinks_ref[h], acc))
    return out


def _store_paired_bias(store, per_head, q_rows):
    for h, b in enumerate(per_head):
        j, g = divmod(h, GROUP)
        store(2 * j + g % 2, (g // 2) * q_rows, b)


def _attend(q_pairs, key_parts, value_parts, bias):
    o = None
    for half in range(2):
        keys = jnp.concatenate(key_parts[half], axis=0)
        vals = jnp.concatenate(value_parts[half], axis=0)
        s = _dot_nt(q_pairs, keys) + bias[half]
        p = jnp.exp(s - jnp.max(s, axis=-1, keepdims=True)).astype(BF16)
        oh = _dot(p, vals)
        o = oh if o is None else o + oh
    return o[:, :LANES] / o[:, LANES:]


def _aux_kernel(xs_ref, meta_ref, g_mix_ref, w_in_ref, conv_w_ref, state_ref, ck_ref, cv_ref,
                cmk_ref, cmv_ref, ids_ref, table_ref, sinks_ref, g_conv_ref, g_attn_ref, w_out_ref,
                h1_ref, sk_ref, sv_ref, sconv_ref, mk2_ref, mv2_ref, mk_ref, mv_ref, muc_ref,
                ucb_ref, q_ref, ya_ref, yc_ref, bias_ref, *, n_batch, seq, n_keys):
    g_mix = g_mix_ref[...]
    w_in = w_in_ref[...]
    tail = jnp.zeros((TAIL_ROWS, LANES), F32)

    mproj = _dot(_rms(meta_ref[...], g_mix).astype(BF16), w_in)
    mk = mproj[:, OFF_K:OFF_K + KV_DIM]
    mv = mproj[:, OFF_V:OFF_V + KV_DIM]
    mk_ref[...] = mk
    mv_ref[...] = mv
    muc_ref[...] = mproj[:, OFF_C:OFF_C + CONV_DIM] * mproj[:, OFF_U:OFF_U + CONV_DIM]
    for iv, var in enumerate(_lane_half_variants(jnp.concatenate([mk, tail], axis=0))):
        mk2_ref[iv] = var
    for iv, var in enumerate(_lane_half_variants(jnp.concatenate([mv, tail], axis=0))):
        mv2_ref[iv] = jnp.concatenate([var, _denominator_lanes(META_ROWS, N_META + 1, iv % 2)], axis=1)

    _store_paired_bias(lambda iv, r0, b: bias_ref.__setitem__((iv, slice(r0, r0 + seq)), b),
                       _build_bias(ids_ref[...], table_ref, sinks_ref), seq)

    xs = xs_ref[...]
    proj = _dot(_rms(xs, g_mix).astype(BF16), w_in)
    uc = proj[:, OFF_C:OFF_C + CONV_DIM] * proj[:, OFF_U:OFF_U + CONV_DIM]
    sk_ref[...] = proj[:, OFF_K:OFF_K + KV_DIM]
    sv_ref[...] = proj[:, OFF_V:OFF_V + KV_DIM]
    q_ref[...] = (proj[:, OFF_Q:OFF_Q + Q_DIM] * (HEAD_DIM ** -0.5)).astype(BF16)
    yc_ref[...] = proj[:, OFF_B:OFF_B + CONV_DIM]
    w0 = conv_w_ref[0:1, :]
    w1 = conv_w_ref[1:2, :]
    w2 = conv_w_ref[2:3, :]
    head = SUBLANES
    for b in range(n_batch):
        ucb_ref[b, head - 2:head, :] = state_ref[b]
        ucb_ref[b, head:head + seq, :] = uc[b * seq:(b + 1) * seq]
        sconv_ref[b] = uc[(b + 1) * seq - 2:(b + 1) * seq]
    den_lanes = [_denominator_lanes(n_keys + TAIL_ROWS, n_keys + 1, half) for half in range(2)]

    def batch_body(b, carry):
        r0 = pl.multiple_of(b * seq, seq)
        rows = pl.ds(r0, seq)
        conv = (w0 * ucb_ref[b, head - 2:head - 2 + seq, :] + w1 * ucb_ref[b, head - 1:head - 1 + seq, :]
                + w2 * ucb_ref[b, head:head + seq, :])
        yc_ref[rows, :] = yc_ref[rows, :] * conv
        k_var = _lane_half_variants(jnp.concatenate([cmk_ref[b], ck_ref[b], sk_ref[rows, :], tail], axis=0))
        v_var = _lane_half_variants(jnp.concatenate([cmv_ref[b], cv_ref[b], sv_ref[rows, :], tail], axis=0))
        for j in range(N_KV_HEADS):
            c0 = j * GROUP * HEAD_DIM
            q_pairs = jnp.concatenate([q_ref[rows, c0:c0 + LANES], q_ref[rows, c0 + LANES:c0 + 2 * LANES]],
                                      axis=0)
            key_parts = [[k_var[2 * j + half]] for half in range(2)]
            value_parts = [[jnp.concatenate([v_var[2 * j + half], den_lanes[half]], axis=1)]
                           for half in range(2)]
            o = _attend(q_pairs, key_parts, value_parts, [bias_ref[2 * j + half] for half in range(2)])
            ya_ref[rows, c0:c0 + LANES] = o[0:seq]
            ya_ref[rows, c0 + LANES:c0 + 2 * LANES] = o[seq:2 * seq]
        return carry

    lax.fori_loop(0, n_batch, batch_body, 0)

    yc_n = _rms(yc_ref[...], g_conv_ref[...]).astype(BF16)
    ya_n = _rms(ya_ref[...], g_attn_ref[...]).astype(BF16)
    h1_ref[...] = xs + _dot(yc_n, w_out_ref[0:CONV_DIM, :]) + _dot(ya_n, w_out_ref[CONV_DIM:, :])


def _smem():
    return pl.BlockSpec(memory_space=pltpu.SMEM)


def _aux_call(xs, meta, g_mix, w_in, conv_w, state, ck, cv, cmk, cmv, ids, table, sinks, g_conv, g_attn, w_out,
              *, n_batch, seq):
    n_tok = n_batch * seq
    n_keys = ids.shape[1] - TAIL_ROWS
    out_shape = (
        jax.ShapeDtypeStruct((n_tok, D_MODEL), F32),
        jax.ShapeDtypeStruct((n_tok, KV_DIM), F32),
        jax.ShapeDtypeStruct((n_tok, KV_DIM), F32),
        jax.ShapeDtypeStruct((n_batch, CONV_W - 1, CONV_DIM), F32),
        jax.ShapeDtypeStruct((N_VARIANTS, META_ROWS, LANES), BF16),
        jax.ShapeDtypeStruct((N_VARIANTS, META_ROWS, 2 * LANES), BF16),
        jax.ShapeDtypeStruct((N_META, KV_DIM), F32),
        jax.ShapeDtypeStruct((N_META, KV_DIM), F32),
        jax.ShapeDtypeStruct((N_META, CONV_DIM), F32),
    )
    vmem = pl.BlockSpec(memory_space=pltpu.VMEM)
    return pl.pallas_call(
        functools.partial(_aux_kernel, n_batch=n_batch, seq=seq, n_keys=n_keys),
        in_specs=[vmem] * 11 + [_smem(), _smem()] + [vmem] * 3,
        out_shape=out_shape,
        scratch_shapes=[
            pltpu.VMEM((n_batch, SUBLANES + seq, CONV_DIM), F32),
            pltpu.VMEM((n_tok, Q_DIM), BF16),
            pltpu.VMEM((n_tok, Q_DIM), F32),
            pltpu.VMEM((n_tok, CONV_DIM), F32),
            pltpu.VMEM((N_VARIANTS, 2 * seq, n_keys + TAIL_ROWS), F32),
        ],
        compiler_params=pltpu.CompilerParams(vmem_limit_bytes=VMEM_LIMIT_BYTES),
        name="aux_sample_mixer",
    )(xs, meta, g_mix, w_in, conv_w, state, ck, cv, cmk, cmv, ids, table, sinks, g_conv, g_attn, w_out)


def _mixer_kernel(x_ref, g_mix_ref, w_in_ref, conv_w_ref, mk2_ref, mv2_ref, muc_ref, ids_ref, table_ref,
                  sinks_ref, g_conv_ref, g_attn_ref, w_out_ref,
                  h1_ref, pk_ref, pv_ref, pc_ref,
                  ucb_ref, kb_ref, vb_ref, q_ref, ya_ref, bias_ref, *, tile, n_tiles):
    b_idx = pl.program_id(0)
    t = pl.program_id(1)
    head = SUBLANES
    chunks = tile // CHUNK

    @pl.when((b_idx == 0) & (t == 0))
    def _():
        for cls in range(WIN_CHUNKS + 1):
            _store_paired_bias(
                lambda iv, r0, b, cls=cls: bias_ref.__setitem__((cls, iv, slice(r0, r0 + CHUNK)), b),
                _build_bias(ids_ref[cls], table_ref, sinks_ref), CHUNK)
        for iv in range(N_VARIANTS):
            vb_ref[iv, :, LANES:] = _denominator_lanes(WINDOW + tile, WINDOW + tile, iv % 2)

    x = x_ref[...]
    proj = _dot(_rms(x, g_mix_ref[...]).astype(BF16), w_in_ref[...])
    uc = proj[:, OFF_C:OFF_C + CONV_DIM] * proj[:, OFF_U:OFF_U + CONV_DIM]
    k = proj[:, OFF_K:OFF_K + KV_DIM]
    v = proj[:, OFF_V:OFF_V + KV_DIM]

    @pl.when(t == 0)
    def _():
        ucb_ref[head - 2:head, :] = muc_ref[N_META - 2:N_META, :]
        zeros = jnp.zeros((N_VARIANTS, WINDOW, LANES), BF16)
        kb_ref[:, 0:WINDOW, :] = zeros
        vb_ref[:, 0:WINDOW, 0:LANES] = zeros

    @pl.when(t > 0)
    def _():
        ucb_ref[head - 2:head, :] = ucb_ref[head + tile - 2:head + tile, :]
        kb_ref[:, 0:WINDOW, :] = kb_ref[:, tile:tile + WINDOW, :]
        vb_ref[:, 0:WINDOW, 0:LANES] = vb_ref[:, tile:tile + WINDOW, 0:LANES]

    ucb_ref[head:head + tile, :] = uc
    for iv, var in enumerate(_lane_half_variants(k)):
        kb_ref[iv, WINDOW:WINDOW + tile, :] = var
    for iv, var in enumerate(_lane_half_variants(v)):
        vb_ref[iv, WINDOW:WINDOW + tile, 0:LANES] = var
    q_ref[...] = (proj[:, OFF_Q:OFF_Q + Q_DIM] * (HEAD_DIM ** -0.5)).astype(BF16)

    conv = (conv_w_ref[0:1, :] * ucb_ref[head - 2:head - 2 + tile, :]
            + conv_w_ref[1:2, :] * ucb_ref[head - 1:head - 1 + tile, :]
            + conv_w_ref[2:3, :] * uc)
    yc = proj[:, OFF_B:OFF_B + CONV_DIM] * conv

    for i in range(chunks):
        cls = jnp.where(t == 0, i, WIN_CHUNKS) if i < WIN_CHUNKS else WIN_CHUNKS
        rows = slice(i * CHUNK, (i + 1) * CHUNK)
        band = slice(i * CHUNK, i * CHUNK + BAND)
        for j in range(N_KV_HEADS):
            c0 = j * GROUP * HEAD_DIM
            q_pairs = jnp.concatenate([q_ref[rows, c0:c0 + LANES], q_ref[rows, c0 + LANES:c0 + 2 * LANES]],
                                      axis=0)
            key_parts = [[kb_ref[2 * j + half, band, :], mk2_ref[2 * j + half]] for half in range(2)]
            value_parts = [[vb_ref[2 * j + half, band, :], mv2_ref[2 * j + half]] for half in range(2)]
            o = _attend(q_pairs, key_parts, value_parts, [bias_ref[cls, 2 * j + half] for half in range(2)])
            ya_ref[rows, c0:c0 + LANES] = o[0:CHUNK]
            ya_ref[rows, c0 + LANES:c0 + 2 * LANES] = o[CHUNK:2 * CHUNK]

    yc_n = _rms(yc, g_conv_ref[...]).astype(BF16)
    ya_n = _rms(ya_ref[...], g_attn_ref[...]).astype(BF16)
    h1_ref[...] = x + _dot(yc_n, w_out_ref[0:CONV_DIM, :]) + _dot(ya_n, w_out_ref[CONV_DIM:, :])

    @pl.when(t == n_tiles - 1)
    def _():
        pk_ref[...] = k[tile - WINDOW:tile]
        pv_ref[...] = v[tile - WINDOW:tile]
        pc_ref[...] = uc[tile - (CONV_W - 1):tile]


def _resident(shape):
    return pl.BlockSpec(shape, lambda *_: (0,) * len(shape), pipeline_mode=pl.Buffered(1))


def _mixer_call(x, g_mix, w_in, conv_w, mk2, mv2, muc, ids, table, sinks, g_conv, g_attn, w_out):
    n_batch, seq, _ = x.shape
    tile = MIXER_TILE
    n_tiles = seq // tile
    assert seq % tile == 0 and tile % CHUNK == 0 and tile >= WINDOW
    n_keys = BAND + META_ROWS
    out_shape = (
        jax.ShapeDtypeStruct((n_batch, seq, D_MODEL), F32),
        jax.ShapeDtypeStruct((n_batch, WINDOW, KV_DIM), F32),
        jax.ShapeDtypeStruct((n_batch, WINDOW, KV_DIM), F32),
        jax.ShapeDtypeStruct((n_batch, CONV_W - 1, CONV_DIM), F32),
    )
    per_batch = lambda shape: pl.BlockSpec((None,) + shape, lambda b, t: (b, 0, 0))
    return pl.pallas_call(
        functools.partial(_mixer_kernel, tile=tile, n_tiles=n_tiles),
        grid=(n_batch, n_tiles),
        in_specs=[
            pl.BlockSpec((None, tile, D_MODEL), lambda b, t: (b, t, 0)),
            _resident(g_mix.shape), _resident(w_in.shape), _resident(conv_w.shape),
            _resident(mk2.shape), _resident(mv2.shape), _resident(muc.shape), _resident(ids.shape),
            _smem(), _smem(),
            _resident(g_conv.shape), _resident(g_attn.shape), _resident(w_out.shape),
        ],
        out_specs=(
            pl.BlockSpec((None, tile, D_MODEL), lambda b, t: (b, t, 0)),
            per_batch((WINDOW, KV_DIM)), per_batch((WINDOW, KV_DIM)), per_batch((CONV_W - 1, CONV_DIM)),
        ),
        out_shape=out_shape,
        scratch_shapes=[
            pltpu.VMEM((SUBLANES + tile, CONV_DIM), F32),
            pltpu.VMEM((N_VARIANTS, WINDOW + tile, LANES), BF16),
            pltpu.VMEM((N_VARIANTS, WINDOW + tile, 2 * LANES), BF16),
            pltpu.VMEM((tile, Q_DIM), BF16),
            pltpu.VMEM((tile, Q_DIM), F32),
            pltpu.VMEM((WIN_CHUNKS + 1, N_VARIANTS, 2 * CHUNK, n_keys), F32),
        ],
        compiler_params=pltpu.CompilerParams(
            dimension_semantics=("arbitrary", "arbitrary"), vmem_limit_bytes=VMEM_LIMIT_BYTES),
        name="prompt_mixer",
    )(x, g_mix, w_in, conv_w, mk2, mv2, muc, ids, table, sinks, g_conv, g_attn, w_out)


def _mlp_kernel(h_ref, g_mlp_ref, w_up_ref, w_down_ref, g_final_ref, y_ref):
    h = h_ref[...]
    u = jnp.maximum(_dot(_rms(h, g_mlp_ref[...]).astype(BF16), w_up_ref[...]), 0.0)
    h2 = h + _dot((u * u).astype(BF16), w_down_ref[...])
    y_ref[...] = _rms(h2, g_final_ref[...])


def _mlp_call(h, g_mlp, w_up, w_down, g_final):
    n_tok = h.shape[0]
    tile = min(MLP_TILE, n_tok)
    assert n_tok % tile == 0
    return pl.pallas_call(
        _mlp_kernel,
        grid=(n_tok // tile,),
        in_specs=[
            pl.BlockSpec((tile, D_MODEL), lambda i: (i, 0)),
            _resident(g_mlp.shape), _resident(w_up.shape), _resident(w_down.shape), _resident(g_final.shape),
        ],
        out_specs=pl.BlockSpec((tile, D_MODEL), lambda i: (i, 0)),
        out_shape=jax.ShapeDtypeStruct((n_tok, D_MODEL), F32),
        compiler_params=pltpu.CompilerParams(
            dimension_semantics=("arbitrary",), vmem_limit_bytes=VMEM_LIMIT_BYTES),
        name="mlp",
    )(h, g_mlp, w_up, w_down, g_final)


def _t5_bucket(rp):
    nb = N_BUCKETS // 2
    max_exact = nb // 2
    ret = jnp.where(rp > 0, nb, 0)
    n = jnp.abs(rp)
    nf = jnp.maximum(n, 1).astype(F32)
    large = max_exact + (jnp.log(nf / max_exact) / math.log(MAX_DISTANCE / max_exact)
                         * (nb - max_exact)).astype(jnp.int32)
    large = jnp.minimum(large, nb - 1)
    return ret + jnp.where(n < max_exact, n, large)


def _bucket_ids(q_pos, k_pos, valid=None):
    ids = _t5_bucket(k_pos[None, :] - q_pos[:, None])
    if valid is not None:
        ids = jnp.where(valid[None, :], ids, MASKED_ID)
    n_q = q_pos.shape[0]
    tail = jnp.concatenate([jnp.full((n_q, 1), SINK_ID, jnp.int32),
                            jnp.full((n_q, TAIL_ROWS - 1), MASKED_ID, jnp.int32)], axis=1)
    return jnp.concatenate([ids, tail], axis=1)


def _prompt_bucket_ids():
    qi = jnp.arange(CHUNK, dtype=jnp.int32)
    r = jnp.arange(BAND, dtype=jnp.int32)
    pm = jnp.arange(N_META, dtype=jnp.int32)
    out = []
    for c in range(WIN_CHUNKS + 1):
        frame_k = c * CHUNK - WIN_CHUNKS * CHUNK + r
        k_pos = jnp.concatenate([N_META + frame_k, pm])
        valid = jnp.concatenate([frame_k >= 0, jnp.ones((N_META,), bool)])
        out.append(_bucket_ids(N_META + c * CHUNK + qi, k_pos, valid))
    return jnp.stack(out)


def _sample_bucket_ids(n_win, seq):
    s = jnp.arange(seq, dtype=jnp.int32)
    k_pos = jnp.concatenate([jnp.arange(N_META, dtype=jnp.int32),
                             N_META + PAST_LEN - n_win + jnp.arange(n_win, dtype=jnp.int32),
                             N_META + PAST_LEN + s])
    return _bucket_ids(N_META + PAST_LEN + s, k_pos)


def kernel(x_prompt, x_sample, cache_k, cache_v, cache_meta_k, cache_meta_v, state_conv, meta_tokens,
           norm_mix, w_in, conv_w, attn_sinks, rel_bias_table, norm_conv_out, norm_attn_out, w_out,
           norm_mlp, w_up, w_down, norm_final):
    n_batch, seq, _ = x_prompt.shape
    s_batch, s_seq, _ = x_sample.shape
    n_win = cache_k.shape[2]
    row = lambda a: a.reshape(1, -1)

    w_in_b = w_in[0].astype(BF16)
    w_out_b = w_out[0].astype(BF16)
    w_up_b = w_up[0].astype(BF16)
    w_down_b = w_down[0].astype(BF16)
    g_mix, g_conv, g_attn = row(norm_mix[0]), row(norm_conv_out[0]), row(norm_attn_out[0])
    g_mlp, g_final = row(norm_mlp[0]), row(norm_final)
    table = rel_bias_table.astype(F32)
    sinks = attn_sinks[0].astype(F32)

    h1_s, s_k, s_v, s_conv, mk2, mv2, mk, mv, muc = _aux_call(
        x_sample.reshape(s_batch * s_seq, D_MODEL), meta_tokens, g_mix, w_in_b, conv_w[0], state_conv[0],
        cache_k[0].reshape(s_batch, n_win, KV_DIM), cache_v[0].reshape(s_batch, n_win, KV_DIM),
        cache_meta_k[0].reshape(s_batch, N_META, KV_DIM), cache_meta_v[0].reshape(s_batch, N_META, KV_DIM),
        _sample_bucket_ids(n_win, s_seq), table, sinks, g_conv, g_attn, w_out_b, n_batch=s_batch, seq=s_seq)

    h1_p, p_k, p_v, p_conv = _mixer_call(
        x_prompt, g_mix, w_in_b, conv_w[0], mk2, mv2, muc, _prompt_bucket_ids(), table, sinks,
        g_conv, g_attn, w_out_b)

    y_prompt = _mlp_call(h1_p.reshape(n_batch * seq, D_MODEL), g_mlp, w_up_b, w_down_b, g_final)
    y_sample = _mlp_call(h1_s, g_mlp, w_up_b, w_down_b, g_final)

    kv_shape = lambda a, n, length: a.reshape(1, n, length, N_KV_HEADS, HEAD_DIM)
    meta_shape = (1, n_batch, N_META, N_KV_HEADS, HEAD_DIM)
    return (
        y_prompt.reshape(n_batch, seq, D_MODEL),
        y_sample.reshape(s_batch, s_seq, D_MODEL),
        kv_shape(p_k, n_batch, WINDOW), kv_shape(p_v, n_batch, WINDOW),
        jnp.broadcast_to(mk.reshape(1, 1, N_META, N_KV_HEADS, HEAD_DIM), meta_shape),
        jnp.broadcast_to(mv.reshape(1, 1, N_META, N_KV_HEADS, HEAD_DIM), meta_shape),
        p_conv[None],
        kv_shape(s_k, s_batch, s_seq), kv_shape(s_v, s_batch, s_seq),
        s_conv[None],
    )
```

```python
import functools
import math

import jax
import jax.numpy as jnp
from jax import lax
from jax.experimental import pallas as pl
from jax.experimental.pallas import tpu as pltpu

D_MODEL = 1024
CHUNK = 64
N_META = 16
CONV_DIM = 512
CONV_W = 3
N_HEADS = 8
N_KV_HEADS = 2
HEAD_DIM = 64
GROUP = N_HEADS // N_KV_HEADS
Q_DIM = N_HEADS * HEAD_DIM
KV_DIM = N_KV_HEADS * HEAD_DIM
IN_DIM = 3 * CONV_DIM + Q_DIM + 2 * KV_DIM
WINDOW = 128
WIN_CHUNKS = WINDOW // CHUNK
BAND = (WIN_CHUNKS + 1) * CHUNK
N_BUCKETS = 32
MAX_DISTANCE = 128
D_FF = 4 * D_MODEL
EPS = 1e-6
PAST_LEN = 4096

OFF_B, OFF_C, OFF_U = 0, CONV_DIM, 2 * CONV_DIM
OFF_Q = 3 * CONV_DIM
OFF_K = OFF_Q + Q_DIM
OFF_V = OFF_K + KV_DIM

LANES = 128
SUBLANES = 8
BF16_ROWS = 16
VMEM_LIMIT_BYTES = 56 * 1024 * 1024
MASK_VALUE = -1e30

LAYER_TILE = 256
MLP_TILE = 512
N_VARIANTS = 2 * N_KV_HEADS
TAIL_ROWS = BF16_ROWS
META_ROWS = N_META + TAIL_ROWS
MASKED_ID = -1
SINK_ID = -2

BF16 = jnp.bfloat16
F32 = jnp.float32


def _rms(x, g):
    ms = jnp.mean(x * x, axis=-1, keepdims=True)
    return x * lax.rsqrt(ms + EPS) * g


def _dot(a, b):
    return jnp.dot(a, b, preferred_element_type=F32)


def _dot_nt(a, b):
    return lax.dot_general(a, b, (((1,), (1,)), ((), ())), preferred_element_type=F32)


def _lane_half_variants(kv):
    lane = lax.broadcasted_iota(jnp.int32, kv.shape, 1)
    low = lane < HEAD_DIM
    swapped = pltpu.roll(kv, HEAD_DIM, axis=1)
    zero = jnp.zeros_like(kv)
    out = (jnp.where(low, kv, zero), jnp.where(low, zero, swapped),
           jnp.where(low, swapped, zero), jnp.where(low, zero, kv))
    return tuple(o.astype(BF16) for o in out)


def _denominator_lanes(n_rows, n_counted, half):
    row = lax.broadcasted_iota(jnp.int32, (n_rows, LANES), 0)
    lane = lax.broadcasted_iota(jnp.int32, (n_rows, LANES), 1)
    in_half = (lane < HEAD_DIM) if half == 0 else (lane >= HEAD_DIM)
    return jnp.where(in_half & (row < n_counted), 1.0, 0.0).astype(BF16)


def _build_bias(ids, table_ref, sinks_ref):
    out = []
    for h in range(N_HEADS):
        acc = jnp.full(ids.shape, MASK_VALUE, F32)
        for bucket in range(N_BUCKETS):
            acc = jnp.where(ids == bucket, table_ref[bucket, h], acc)
        out.append(jnp.where(ids == SINK_ID, sinks_ref[h], acc))
    return out


def _store_paired_bias(store, per_head, q_rows):
    for h, b in enumerate(per_head):
        j, g = divmod(h, GROUP)
        store(2 * j + g % 2, (g // 2) * q_rows, b)


def _attend_scores(q_pairs, key_parts, bias):
    probs = []
    for half in range(2):
        s = _dot_nt(q_pairs, jnp.concatenate(key_parts[half], axis=0)) + bias[half]
        probs.append(jnp.exp(s - jnp.max(s, axis=-1, keepdims=True)).astype(BF16))
    return probs


def _attend_values(probs, value_parts):
    o = (_dot(probs[0], jnp.concatenate(value_parts[0], axis=0))
         + _dot(probs[1], jnp.concatenate(value_parts[1], axis=0)))
    return o[:, :LANES] / o[:, LANES:]


def _attend(q_pairs, key_parts, value_parts, bias):
    return _attend_values(_attend_scores(q_pairs, key_parts, bias), value_parts)


def _aux_kernel(xs_ref, meta_ref, g_mix_ref, w_in_ref, conv_w_ref, state_ref, ck_ref, cv_ref,
                cmk_ref, cmv_ref, ids_ref, table_ref, sinks_ref, g_conv_ref, g_attn_ref, w_out_ref,
                h1_ref, sk_ref, sv_ref, sconv_ref, mk2_ref, mv2_ref, mk_ref, mv_ref, muc_ref,
                ucb_ref, q_ref, ya_ref, yc_ref, bias_ref, *, n_batch, seq, n_keys):
    g_mix = g_mix_ref[...]
    w_in = w_in_ref[...]
    tail = jnp.zeros((TAIL_ROWS, LANES), F32)

    mproj = _dot(_rms(meta_ref[...], g_mix).astype(BF16), w_in)
    mk = mproj[:, OFF_K:OFF_K + KV_DIM]
    mv = mproj[:, OFF_V:OFF_V + KV_DIM]
    mk_ref[...] = mk
    mv_ref[...] = mv
    muc_ref[...] = mproj[:, OFF_C:OFF_C + CONV_DIM] * mproj[:, OFF_U:OFF_U + CONV_DIM]
    for iv, var in enumerate(_lane_half_variants(jnp.concatenate([mk, tail], axis=0))):
        mk2_ref[iv] = var
    for iv, var in enumerate(_lane_half_variants(jnp.concatenate([mv, tail], axis=0))):
        mv2_ref[iv] = jnp.concatenate([var, _denominator_lanes(META_ROWS, N_META + 1, iv % 2)], axis=1)

    _store_paired_bias(lambda iv, r0, b: bias_ref.__setitem__((iv, slice(r0, r0 + seq)), b),
                       _build_bias(ids_ref[...], table_ref, sinks_ref), seq)

    xs = xs_ref[...]
    proj = _dot(_rms(xs, g_mix).astype(BF16), w_in)
    uc = proj[:, OFF_C:OFF_C + CONV_DIM] * proj[:, OFF_U:OFF_U + CONV_DIM]
    sk_ref[...] = proj[:, OFF_K:OFF_K + KV_DIM]
    sv_ref[...] = proj[:, OFF_V:OFF_V + KV_DIM]
    q_ref[...] = (proj[:, OFF_Q:OFF_Q + Q_DIM] * (HEAD_DIM ** -0.5)).astype(BF16)
    yc_ref[...] = proj[:, OFF_B:OFF_B + CONV_DIM]
    w0 = conv_w_ref[0:1, :]
    w1 = conv_w_ref[1:2, :]
    w2 = conv_w_ref[2:3, :]
    head = SUBLANES
    for b in range(n_batch):
        ucb_ref[b, head - 2:head, :] = state_ref[b]
        ucb_ref[b, head:head + seq, :] = uc[b * seq:(b + 1) * seq]
        sconv_ref[b] = uc[(b + 1) * seq - 2:(b + 1) * seq]
    den_lanes = [_denominator_lanes(n_keys + TAIL_ROWS, n_keys + 1, half) for half in range(2)]

    def batch_body(b, carry):
        r0 = pl.multiple_of(b * seq, seq)
        rows = pl.ds(r0, seq)
        conv = (w0 * ucb_ref[b, head - 2:head - 2 + seq, :] + w1 * ucb_ref[b, head - 1:head - 1 + seq, :]
                + w2 * ucb_ref[b, head:head + seq, :])
        yc_ref[rows, :] = yc_ref[rows, :] * conv
        k_var = _lane_half_variants(jnp.concatenate([cmk_ref[b], ck_ref[b], sk_ref[rows, :], tail], axis=0))
        v_var = _lane_half_variants(jnp.concatenate([cmv_ref[b], cv_ref[b], sv_ref[rows, :], tail], axis=0))
        for j in range(N_KV_HEADS):
            c0 = j * GROUP * HEAD_DIM
            q_pairs = jnp.concatenate([q_ref[rows, c0:c0 + LANES], q_ref[rows, c0 + LANES:c0 + 2 * LANES]],
                                      axis=0)
            key_parts = [[k_var[2 * j + half]] for half in range(2)]
            value_parts = [[jnp.concatenate([v_var[2 * j + half], den_lanes[half]], axis=1)]
                           for half in range(2)]
            o = _attend(q_pairs, key_parts, value_parts, [bias_ref[2 * j + half] for half in range(2)])
            ya_ref[rows, c0:c0 + LANES] = o[0:seq]
            ya_ref[rows, c0 + LANES:c0 + 2 * LANES] = o[seq:2 * seq]
        return carry

    lax.fori_loop(0, n_batch, batch_body, 0)

    yc_n = _rms(yc_ref[...], g_conv_ref[...]).astype(BF16)
    ya_n = _rms(ya_ref[...], g_attn_ref[...]).astype(BF16)
    h1_ref[...] = xs + _dot(yc_n, w_out_ref[0:CONV_DIM, :]) + _dot(ya_n, w_out_ref[CONV_DIM:, :])


def _smem():
    return pl.BlockSpec(memory_space=pltpu.SMEM)


def _aux_call(xs, meta, g_mix, w_in, conv_w, state, ck, cv, cmk, cmv, ids, table, sinks, g_conv, g_attn, w_out,
              *, n_batch, seq):
    n_tok = n_batch * seq
    n_keys = ids.shape[1] - TAIL_ROWS
    out_shape = (
        jax.ShapeDtypeStruct((n_tok, D_MODEL), F32),
        jax.ShapeDtypeStruct((n_tok, KV_DIM), F32),
        jax.ShapeDtypeStruct((n_tok, KV_DIM), F32),
        jax.ShapeDtypeStruct((n_batch, CONV_W - 1, CONV_DIM), F32),
        jax.ShapeDtypeStruct((N_VARIANTS, META_ROWS, LANES), BF16),
        jax.ShapeDtypeStruct((N_VARIANTS, META_ROWS, 2 * LANES), BF16),
        jax.ShapeDtypeStruct((N_META, KV_DIM), F32),
        jax.ShapeDtypeStruct((N_META, KV_DIM), F32),
        jax.ShapeDtypeStruct((N_META, CONV_DIM), F32),
    )
    vmem = pl.BlockSpec(memory_space=pltpu.VMEM)
    return pl.pallas_call(
        functools.partial(_aux_kernel, n_batch=n_batch, seq=seq, n_keys=n_keys),
        in_specs=[vmem] * 11 + [_smem(), _smem()] + [vmem] * 3,
        out_shape=out_shape,
        scratch_shapes=[
            pltpu.VMEM((n_batch, SUBLANES + seq, CONV_DIM), F32),
            pltpu.VMEM((n_tok, Q_DIM), BF16),
            pltpu.VMEM((n_tok, Q_DIM), F32),
            pltpu.VMEM((n_tok, CONV_DIM), F32),
            pltpu.VMEM((N_VARIANTS, 2 * seq, n_keys + TAIL_ROWS), F32),
        ],
        compiler_params=pltpu.CompilerParams(vmem_limit_bytes=VMEM_LIMIT_BYTES),
        name="aux_sample_mixer",
    )(xs, meta, g_mix, w_in, conv_w, state, ck, cv, cmk, cmv, ids, table, sinks, g_conv, g_attn, w_out)


def _layer_kernel(x_ref, g_mix_ref, w_in_ref, conv_w_ref, mk2_ref, mv2_ref, muc_ref, ids_ref, table_ref,
                  sinks_ref, g_conv_ref, g_attn_ref, w_out_ref, g_mlp_ref, w_up_ref, w_down_ref, g_final_ref,
                  y_ref, pk_ref, pv_ref, pc_ref,
                  ucb_ref, kb_ref, vb_ref, q_ref, ya_ref, bias_ref, h1_ref, *, tile, n_tiles, n_steps):
    s = pl.program_id(0)
    t = lax.rem(jnp.minimum(s, n_steps - 1), n_tiles)
    first = t == 0
    slot = lax.rem(s, 2)
    head = SUBLANES
    chunks = tile // CHUNK

    @pl.when(s == 0)
    def _():
        for cls in range(WIN_CHUNKS + 1):
            _store_paired_bias(
                lambda iv, r0, b, cls=cls: bias_ref.__setitem__((cls, iv, slice(r0, r0 + CHUNK)), b),
                _build_bias(ids_ref[cls], table_ref, sinks_ref), CHUNK)
        ucb_ref[...] = jnp.zeros(ucb_ref.shape, F32)
        kb_ref[...] = jnp.zeros(kb_ref.shape, BF16)
        vb_ref[:, :, 0:LANES] = jnp.zeros((N_VARIANTS, WINDOW + tile, LANES), BF16)
        for iv in range(N_VARIANTS):
            vb_ref[iv, :, LANES:] = _denominator_lanes(WINDOW + tile, WINDOW + tile, iv % 2)
        h1_ref[...] = jnp.zeros(h1_ref.shape, F32)

    x = x_ref[...]
    proj = _dot(_rms(x, g_mix_ref[...]).astype(BF16), w_in_ref[...])
    uc = proj[:, OFF_C:OFF_C + CONV_DIM] * proj[:, OFF_U:OFF_U + CONV_DIM]
    k = proj[:, OFF_K:OFF_K + KV_DIM]
    v = proj[:, OFF_V:OFF_V + KV_DIM]

    ucb_ref[head - 2:head, :] = jnp.where(first, muc_ref[N_META - 2:N_META, :],
                                          ucb_ref[head + tile - 2:head + tile, :])
    zeros = jnp.zeros((N_VARIANTS, WINDOW, LANES), BF16)
    kb_ref[:, 0:WINDOW, :] = jnp.where(first, zeros, kb_ref[:, tile:tile + WINDOW, :])
    vb_ref[:, 0:WINDOW, 0:LANES] = jnp.where(first, zeros, vb_ref[:, tile:tile + WINDOW, 0:LANES])

    ucb_ref[head:head + tile, :] = uc
    for iv, var in enumerate(_lane_half_variants(k)):
        kb_ref[iv, WINDOW:WINDOW + tile, :] = var
    for iv, var in enumerate(_lane_half_variants(v)):
        vb_ref[iv, WINDOW:WINDOW + tile, 0:LANES] = var
    q_ref[...] = (proj[:, OFF_Q:OFF_Q + Q_DIM] * (HEAD_DIM ** -0.5)).astype(BF16)

    conv = (conv_w_ref[0:1, :] * ucb_ref[head - 2:head - 2 + tile, :]
            + conv_w_ref[1:2, :] * ucb_ref[head - 1:head - 1 + tile, :]
            + conv_w_ref[2:3, :] * uc)
    yc = proj[:, OFF_B:OFF_B + CONV_DIM] * conv

    h = h1_ref[1 - slot]
    hn = _rms(h, g_mlp_ref[...]).astype(BF16)
    h2 = h
    piece = D_FF // chunks

    for i in range(chunks):
        cls = jnp.where(first, i, WIN_CHUNKS) if i < WIN_CHUNKS else WIN_CHUNKS
        rows = slice(i * CHUNK, (i + 1) * CHUNK)
        band = slice(i * CHUNK, i * CHUNK + BAND)
        probs = []
        for j in range(N_KV_HEADS):
            c0 = j * GROUP * HEAD_DIM
            q_pairs = jnp.concatenate([q_ref[rows, c0:c0 + LANES], q_ref[rows, c0 + LANES:c0 + 2 * LANES]],
                                      axis=0)
            key_parts = [[kb_ref[2 * j + half, band, :], mk2_ref[2 * j + half]] for half in range(2)]
            probs.append(_attend_scores(q_pairs, key_parts, [bias_ref[cls, 2 * j + half] for half in range(2)]))

        n0 = i * piece
        u = jnp.maximum(_dot(hn, w_up_ref[:, n0:n0 + piece]), 0.0)
        h2 = h2 + _dot((u * u).astype(BF16), w_down_ref[n0:n0 + piece, :])

        for j in range(N_KV_HEADS):
            c0 = j * GROUP * HEAD_DIM
            value_parts = [[vb_ref[2 * j + half, band, :], mv2_ref[2 * j + half]] for half in range(2)]
            o = _attend_values(probs[j], value_parts)
            ya_ref[rows, c0:c0 + LANES] = o[0:CHUNK]
            ya_ref[rows, c0 + LANES:c0 + 2 * LANES] = o[CHUNK:2 * CHUNK]

    y_ref[...] = _rms(h2, g_final_ref[...])

    yc_n = _rms(yc, g_conv_ref[...]).astype(BF16)
    ya_n = _rms(ya_ref[...], g_attn_ref[...]).astype(BF16)
    h1_ref[slot] = x + _dot(yc_n, w_out_ref[0:CONV_DIM, :]) + _dot(ya_n, w_out_ref[CONV_DIM:, :])

    pk_ref[...] = k[tile - WINDOW:tile]
    pv_ref[...] = v[tile - WINDOW:tile]
    pc_ref[...] = uc[tile - (CONV_W - 1):tile]


def _resident(shape):
    return pl.BlockSpec(shape, lambda *_: (0,) * len(shape), pipeline_mode=pl.Buffered(1))


def _layer_call(x, g_mix, w_in, conv_w, mk2, mv2, muc, ids, table, sinks, g_conv, g_attn, w_out,
                g_mlp, w_up, w_down, g_final):
    n_batch, seq, _ = x.shape
    tile = LAYER_TILE
    n_tiles = seq // tile
    n_steps = n_batch * n_tiles
    assert seq % tile == 0 and tile % CHUNK == 0 and tile >= WINDOW
    n_keys = BAND + META_ROWS
    out_shape = (
        jax.ShapeDtypeStruct((n_batch, seq, D_MODEL), F32),
        jax.ShapeDtypeStruct((n_batch, WINDOW, KV_DIM), F32),
        jax.ShapeDtypeStruct((n_batch, WINDOW, KV_DIM), F32),
        jax.ShapeDtypeStruct((n_batch, CONV_W - 1, CONV_DIM), F32),
    )

    def mixer_block(s):
        s = jnp.minimum(s, n_steps - 1)
        return s // n_tiles, s % n_tiles, 0

    def mlp_block(s):
        s = jnp.maximum(s - 1, 0)
        return s // n_tiles, s % n_tiles, 0

    per_batch = lambda shape: pl.BlockSpec((None,) + shape, lambda s: (mixer_block(s)[0], 0, 0))
    operands = (x, g_mix, w_in, conv_w, mk2, mv2, muc, ids, table, sinks, g_conv, g_attn, w_out,
                g_mlp, w_up, w_down, g_final)
    in_specs = [pl.BlockSpec((None, tile, D_MODEL), mixer_block)]
    in_specs += [_smem() if a is table or a is sinks else _resident(a.shape) for a in operands[1:]]
    return pl.pallas_call(
        functools.partial(_layer_kernel, tile=tile, n_tiles=n_tiles, n_steps=n_steps),
        grid=(n_steps + 1,),
        in_specs=in_specs,
        out_specs=(
            pl.BlockSpec((None, tile, D_MODEL), mlp_block),
            per_batch((WINDOW, KV_DIM)), per_batch((WINDOW, KV_DIM)), per_batch((CONV_W - 1, CONV_DIM)),
        ),
        out_shape=out_shape,
        scratch_shapes=[
            pltpu.VMEM((SUBLANES + tile, CONV_DIM), F32),
            pltpu.VMEM((N_VARIANTS, WINDOW + tile, LANES), BF16),
            pltpu.VMEM((N_VARIANTS, WINDOW + tile, 2 * LANES), BF16),
            pltpu.VMEM((tile, Q_DIM), BF16),
            pltpu.VMEM((tile, Q_DIM), F32),
            pltpu.VMEM((WIN_CHUNKS + 1, N_VARIANTS, 2 * CHUNK, n_keys), F32),
            pltpu.VMEM((2, tile, D_MODEL), F32),
        ],
        compiler_params=pltpu.CompilerParams(
            dimension_semantics=("arbitrary",), vmem_limit_bytes=VMEM_LIMIT_BYTES),
        name="prompt_layer",
    )(*operands)


def _mlp_kernel(h_ref, g_mlp_ref, w_up_ref, w_down_ref, g_final_ref, y_ref):
    h = h_ref[...]
    u = jnp.maximum(_dot(_rms(h, g_mlp_ref[...]).astype(BF16), w_up_ref[...]), 0.0)
    h2 = h + _dot((u * u).astype(BF16), w_down_ref[...])
    y_ref[...] = _rms(h2, g_final_ref[...])


def _mlp_call(h, g_mlp, w_up, w_down, g_final):
    n_tok = h.shape[0]
    tile = min(MLP_TILE, n_tok)
    assert n_tok % tile == 0
    return pl.pallas_call(
        _mlp_kernel,
        grid=(n_tok // tile,),
        in_specs=[
            pl.BlockSpec((tile, D_MODEL), lambda i: (i, 0)),
            _resident(g_mlp.shape), _resident(w_up.shape), _resident(w_down.shape), _resident(g_final.shape),
        ],
        out_specs=pl.BlockSpec((tile, D_MODEL), lambda i: (i, 0)),
        out_shape=jax.ShapeDtypeStruct((n_tok, D_MODEL), F32),
        compiler_params=pltpu.CompilerParams(
            dimension_semantics=("arbitrary",), vmem_limit_bytes=VMEM_LIMIT_BYTES),
        name="mlp",
    )(h, g_mlp, w_up, w_down, g_final)


def _t5_bucket(rp):
    nb = N_BUCKETS // 2
    max_exact = nb // 2
    ret = jnp.where(rp > 0, nb, 0)
    n = jnp.abs(rp)
    nf = jnp.maximum(n, 1).astype(F32)
    large = max_exact + (jnp.log(nf / max_exact) / math.log(MAX_DISTANCE / max_exact)
                         * (nb - max_exact)).astype(jnp.int32)
    large = jnp.minimum(large, nb - 1)
    return ret + jnp.where(n < max_exact, n, large)


def _bucket_ids(q_pos, k_pos, valid=None):
    ids = _t5_bucket(k_pos[None, :] - q_pos[:, None])
    if valid is not None:
        ids = jnp.where(valid[None, :], ids, MASKED_ID)
    n_q = q_pos.shape[0]
    tail = jnp.concatenate([jnp.full((n_q, 1), SINK_ID, jnp.int32),
                            jnp.full((n_q, TAIL_ROWS - 1), MASKED_ID, jnp.int32)], axis=1)
    return jnp.concatenate([ids, tail], axis=1)


def _prompt_bucket_ids():
    qi = jnp.arange(CHUNK, dtype=jnp.int32)
    r = jnp.arange(BAND, dtype=jnp.int32)
    pm = jnp.arange(N_META, dtype=jnp.int32)
    out = []
    for c in range(WIN_CHUNKS + 1):
        frame_k = c * CHUNK - WIN_CHUNKS * CHUNK + r
        k_pos = jnp.concatenate([N_META + frame_k, pm])
        valid = jnp.concatenate([frame_k >= 0, jnp.ones((N_META,), bool)])
        out.append(_bucket_ids(N_META + c * CHUNK + qi, k_pos, valid))
    return jnp.stack(out)


def _sample_bucket_ids(n_win, seq):
    s = jnp.arange(seq, dtype=jnp.int32)
    k_pos = jnp.concatenate([jnp.arange(N_META, dtype=jnp.int32),
                             N_META + PAST_LEN - n_win + jnp.arange(n_win, dtype=jnp.int32),
                             N_META + PAST_LEN + s])
    return _bucket_ids(N_META + PAST_LEN + s, k_pos)


def kernel(x_prompt, x_sample, cache_k, cache_v, cache_meta_k, cache_meta_v, state_conv, meta_tokens,
           norm_mix, w_in, conv_w, attn_sinks, rel_bias_table, norm_conv_out, norm_attn_out, w_out,
           norm_mlp, w_up, w_down, norm_final):
    n_batch, seq, _ = x_prompt.shape
    s_batch, s_seq, _ = x_sample.shape
    n_win = cache_k.shape[2]
    row = lambda a: a.reshape(1, -1)

    w_in_b = w_in[0].astype(BF16)
    w_out_b = w_out[0].astype(BF16)
    w_up_b = w_up[0].astype(BF16)
    w_down_b = w_down[0].astype(BF16)
    g_mix, g_conv, g_attn = row(norm_mix[0]), row(norm_conv_out[0]), row(norm_attn_out[0])
    g_mlp, g_final = row(norm_mlp[0]), row(norm_final)
    table = rel_bias_table.astype(F32)
    sinks = attn_sinks[0].astype(F32)

    h1_s, s_k, s_v, s_conv, mk2, mv2, mk, mv, muc = _aux_call(
        x_sample.reshape(s_batch * s_seq, D_MODEL), meta_tokens, g_mix, w_in_b, conv_w[0], state_conv[0],
        cache_k[0].reshape(s_batch, n_win, KV_DIM), cache_v[0].reshape(s_batch, n_win, KV_DIM),
        cache_meta_k[0].reshape(s_batch, N_META, KV_DIM), cache_meta_v[0].reshape(s_batch, N_META, KV_DIM),
        _sample_bucket_ids(n_win, s_seq), table, sinks, g_conv, g_attn, w_out_b, n_batch=s_batch, seq=s_seq)

    y_prompt, p_k, p_v, p_conv = _layer_call(
        x_prompt, g_mix, w_in_b, conv_w[0], mk2, mv2, muc, _prompt_bucket_ids(), table, sinks,
        g_conv, g_attn, w_out_b, g_mlp, w_up_b, w_down_b, g_final)
    y_sample = _mlp_call(h1_s, g_mlp, w_up_b, w_down_b, g_final)

    kv_shape = lambda a, n, length: a.reshape(1, n, length, N_KV_HEADS, HEAD_DIM)
    meta_shape = (1, n_batch, N_META, N_KV_HEADS, HEAD_DIM)
    return (
        y_prompt,
        y_sample.reshape(s_batch, s_seq, D_MODEL),
        kv_shape(p_k, n_batch, WINDOW), kv_shape(p_v, n_batch, WINDOW),
        jnp.broadcast_to(mk.reshape(1, 1, N_META, N_KV_HEADS, HEAD_DIM), meta_shape),
        jnp.broadcast_to(mv.reshape(1, 1, N_META, N_KV_HEADS, HEAD_DIM), meta_shape),
        p_conv[None],
        kv_shape(s_k, s_batch, s_seq), kv_shape(s_v, s_batch, s_seq),
        s_conv[None],
    )
```

```python
import functools
import math

import jax
import jax.numpy as jnp
from jax import lax
from jax.experimental import pallas as pl
from jax.experimental.pallas import tpu as pltpu

D_MODEL = 1024
CHUNK = 64
N_META = 16
CONV_DIM = 512
CONV_W = 3
N_HEADS = 8
N_KV_HEADS = 2
HEAD_DIM = 64
GROUP = N_HEADS // N_KV_HEADS
Q_DIM = N_HEADS * HEAD_DIM
KV_DIM = N_KV_HEADS * HEAD_DIM
IN_DIM = 3 * CONV_DIM + Q_DIM + 2 * KV_DIM
WINDOW = 128
WIN_CHUNKS = WINDOW // CHUNK
BAND = (WIN_CHUNKS + 1) * CHUNK
N_BUCKETS = 32
MAX_DISTANCE = 128
D_FF = 4 * D_MODEL
EPS = 1e-6
PAST_LEN = 4096

OFF_B, OFF_C, OFF_U = 0, CONV_DIM, 2 * CONV_DIM
OFF_Q = 3 * CONV_DIM
OFF_K = OFF_Q + Q_DIM
OFF_V = OFF_K + KV_DIM

LANES = 128
SUBLANES = 8
BF16_ROWS = 16
VMEM_LIMIT_BYTES = 56 * 1024 * 1024
MASK_VALUE = -1e30

MXU_TILE = 256
LAYER_TILE = 256
CHUNK_GROUP = 2
MLP_PIECES = 8
MLP_TILE = 512
N_VARIANTS = 2 * N_KV_HEADS
TAIL_ROWS = BF16_ROWS
META_ROWS = N_META + TAIL_ROWS
MASKED_ID = -1
SINK_ID = -2

BF16 = jnp.bfloat16
F32 = jnp.float32


def _rms(x, g):
    ms = jnp.mean(x * x, axis=-1, keepdims=True)
    return x * lax.rsqrt(ms + EPS) * g


def _dot(a, b):
    return jnp.dot(a, b, preferred_element_type=F32)


def _dot_nt(a, b):
    return lax.dot_general(a, b, (((1,), (1,)), ((), ())), preferred_element_type=F32)


def _lane_half_variants(kv):
    lane = lax.broadcasted_iota(jnp.int32, kv.shape, 1)
    low = lane < HEAD_DIM
    swapped = pltpu.roll(kv, HEAD_DIM, axis=1)
    zero = jnp.zeros_like(kv)
    out = (jnp.where(low, kv, zero), jnp.where(low, zero, swapped),
           jnp.where(low, swapped, zero), jnp.where(low, zero, kv))
    return tuple(o.astype(BF16) for o in out)


def _denominator_lanes(n_rows, n_counted, half):
    row = lax.broadcasted_iota(jnp.int32, (n_rows, LANES), 0)
    lane = lax.broadcasted_iota(jnp.int32, (n_rows, LANES), 1)
    in_half = (lane < HEAD_DIM) if half == 0 else (lane >= HEAD_DIM)
    return jnp.where(in_half & (row < n_counted), 1.0, 0.0).astype(BF16)


def _build_bias(ids, table_ref, sinks_ref):
    out = []
    for h in range(N_HEADS):
        acc = jnp.full(ids.shape, MASK_VALUE, F32)
        for bucket in range(N_BUCKETS):
            acc = jnp.where(ids == bucket, table_ref[bucket, h], acc)
        out.append(jnp.where(ids == SINK_ID, sinks_ref[h], acc))
    return out


def _run_lengths(n_items, n_runs):
    base, extra = divmod(n_items, n_runs)
    return [base + (1 if i >= n_runs - extra else 0) for i in range(n_runs)]


def _split_columns(width, n_parts):
    size, rem = divmod(width, n_parts)
    assert rem == 0 and size % MXU_TILE == 0
    return [(i * size, (i + 1) * size) for i in range(n_parts)]


def _store_paired_bias(store, per_head, q_rows):
    for h, b in enumerate(per_head):
        j, g = divmod(h, GROUP)
        store(2 * j + g % 2, (g // 2) * q_rows, b)


def _attend_scores(q_pairs, key_parts, bias):
    probs = []
    for half in range(2):
        s = _dot_nt(q_pairs, jnp.concatenate(key_parts[half], axis=0)) + bias[half]
        probs.append(jnp.exp(s - jnp.max(s, axis=-1, keepdims=True)).astype(BF16))
    return probs


def _attend_values(probs, value_parts):
    o = (_dot(probs[0], jnp.concatenate(value_parts[0], axis=0))
         + _dot(probs[1], jnp.concatenate(value_parts[1], axis=0)))
    return o[:, :LANES] / o[:, LANES:]


def _attend(q_pairs, key_parts, value_parts, bias):
    return _attend_values(_attend_scores(q_pairs, key_parts, bias), value_parts)


def _aux_kernel(xs_ref, meta_ref, g_mix_ref, w_in_ref, conv_w_ref, state_ref, ck_ref, cv_ref,
                cmk_ref, cmv_ref, ids_ref, table_ref, sinks_ref, g_conv_ref, g_attn_ref, w_out_ref,
                h1_ref, sk_ref, sv_ref, sconv_ref, mk2_ref, mv2_ref, mk_ref, mv_ref, muc_ref,
                ucb_ref, q_ref, ya_ref, yc_ref, bias_ref, *, n_batch, seq, n_keys):
    g_mix = g_mix_ref[...]
    w_in = w_in_ref[...]
    tail = jnp.zeros((TAIL_ROWS, LANES), F32)

    mproj = _dot(_rms(meta_ref[...], g_mix).astype(BF16), w_in)
    mk = mproj[:, OFF_K:OFF_K + KV_DIM]
    mv = mproj[:, OFF_V:OFF_V + KV_DIM]
    mk_ref[...] = mk
    mv_ref[...] = mv
    muc_ref[...] = mproj[:, OFF_C:OFF_C + CONV_DIM] * mproj[:, OFF_U:OFF_U + CONV_DIM]
    for iv, var in enumerate(_lane_half_variants(jnp.concatenate([mk, tail], axis=0))):
        mk2_ref[iv] = var
    for iv, var in enumerate(_lane_half_variants(jnp.concatenate([mv, tail], axis=0))):
        mv2_ref[iv] = jnp.concatenate([var, _denominator_lanes(META_ROWS, N_META + 1, iv % 2)], axis=1)

    _store_paired_bias(lambda iv, r0, b: bias_ref.__setitem__((iv, slice(r0, r0 + seq)), b),
                       _build_bias(ids_ref[...], table_ref, sinks_ref), seq)

    xs = xs_ref[...]
    proj = _dot(_rms(xs, g_mix).astype(BF16), w_in)
    uc = proj[:, OFF_C:OFF_C + CONV_DIM] * proj[:, OFF_U:OFF_U + CONV_DIM]
    sk_ref[...] = proj[:, OFF_K:OFF_K + KV_DIM]
    sv_ref[...] = proj[:, OFF_V:OFF_V + KV_DIM]
    q_ref[...] = (proj[:, OFF_Q:OFF_Q + Q_DIM] * (HEAD_DIM ** -0.5)).astype(BF16)
    yc_ref[...] = proj[:, OFF_B:OFF_B + CONV_DIM]
    w0 = conv_w_ref[0:1, :]
    w1 = conv_w_ref[1:2, :]
    w2 = conv_w_ref[2:3, :]
    head = SUBLANES
    for b in range(n_batch):
        ucb_ref[b, head - 2:head, :] = state_ref[b]
        ucb_ref[b, head:head + seq, :] = uc[b * seq:(b + 1) * seq]
        sconv_ref[b] = uc[(b + 1) * seq - 2:(b + 1) * seq]
    den_lanes = [_denominator_lanes(n_keys + TAIL_ROWS, n_keys + 1, half) for half in range(2)]

    def batch_body(b, carry):
        r0 = pl.multiple_of(b * seq, seq)
        rows = pl.ds(r0, seq)
        conv = (w0 * ucb_ref[b, head - 2:head - 2 + seq, :] + w1 * ucb_ref[b, head - 1:head - 1 + seq, :]
                + w2 * ucb_ref[b, head:head + seq, :])
        yc_ref[rows, :] = yc_ref[rows, :] * conv
        k_var = _lane_half_variants(jnp.concatenate([cmk_ref[b], ck_ref[b], sk_ref[rows, :], tail], axis=0))
        v_var = _lane_half_variants(jnp.concatenate([cmv_ref[b], cv_ref[b], sv_ref[rows, :], tail], axis=0))
        for j in range(N_KV_HEADS):
            c0 = j * GROUP * HEAD_DIM
            q_pairs = jnp.concatenate([q_ref[rows, c0:c0 + LANES], q_ref[rows, c0 + LANES:c0 + 2 * LANES]],
                                      axis=0)
            key_parts = [[k_var[2 * j + half]] for half in range(2)]
            value_parts = [[jnp.concatenate([v_var[2 * j + half], den_lanes[half]], axis=1)]
                           for half in range(2)]
            o = _attend(q_pairs, key_parts, value_parts, [bias_ref[2 * j + half] for half in range(2)])
            ya_ref[rows, c0:c0 + LANES] = o[0:seq]
            ya_ref[rows, c0 + LANES:c0 + 2 * LANES] = o[seq:2 * seq]
        return carry

    lax.fori_loop(0, n_batch, batch_body, 0)

    yc_n = _rms(yc_ref[...], g_conv_ref[...]).astype(BF16)
    ya_n = _rms(ya_ref[...], g_attn_ref[...]).astype(BF16)
    h1_ref[...] = xs + _dot(yc_n, w_out_ref[0:CONV_DIM, :]) + _dot(ya_n, w_out_ref[CONV_DIM:, :])


def _smem():
    return pl.BlockSpec(memory_space=pltpu.SMEM)


def _aux_call(xs, meta, g_mix, w_in, conv_w, state, ck, cv, cmk, cmv, ids, table, sinks, g_conv, g_attn, w_out,
              *, n_batch, seq):
    n_tok = n_batch * seq
    n_keys = ids.shape[1] - TAIL_ROWS
    out_shape = (
        jax.ShapeDtypeStruct((n_tok, D_MODEL), F32),
        jax.ShapeDtypeStruct((n_tok, KV_DIM), F32),
        jax.ShapeDtypeStruct((n_tok, KV_DIM), F32),
        jax.ShapeDtypeStruct((n_batch, CONV_W - 1, CONV_DIM), F32),
        jax.ShapeDtypeStruct((N_VARIANTS, META_ROWS, LANES), BF16),
        jax.ShapeDtypeStruct((N_VARIANTS, META_ROWS, 2 * LANES), BF16),
        jax.ShapeDtypeStruct((N_META, KV_DIM), F32),
        jax.ShapeDtypeStruct((N_META, KV_DIM), F32),
        jax.ShapeDtypeStruct((N_META, CONV_DIM), F32),
    )
    vmem = pl.BlockSpec(memory_space=pltpu.VMEM)
    return pl.pallas_call(
        functools.partial(_aux_kernel, n_batch=n_batch, seq=seq, n_keys=n_keys),
        in_specs=[vmem] * 11 + [_smem(), _smem()] + [vmem] * 3,
        out_shape=out_shape,
        scratch_shapes=[
            pltpu.VMEM((n_batch, SUBLANES + seq, CONV_DIM), F32),
            pltpu.VMEM((n_tok, Q_DIM), BF16),
            pltpu.VMEM((n_tok, Q_DIM), F32),
            pltpu.VMEM((n_tok, CONV_DIM), F32),
            pltpu.VMEM((N_VARIANTS, 2 * seq, n_keys + TAIL_ROWS), F32),
        ],
        compiler_params=pltpu.CompilerParams(vmem_limit_bytes=VMEM_LIMIT_BYTES),
        name="aux_sample_mixer",
    )(xs, meta, g_mix, w_in, conv_w, state, ck, cv, cmk, cmv, ids, table, sinks, g_conv, g_attn, w_out)


def _layer_kernel(xnext_ref, xprev_ref, g_mix_ref, w_in_ref, conv_w_ref, mk2_ref, mv2_ref, muc_ref, ids_ref,
                  table_ref, sinks_ref, g_conv_ref, g_attn_ref, w_out_ref, g_mlp_ref, w_up_ref, w_down_ref,
                  g_final_ref,
                  y_ref, pk_ref, pv_ref, pc_ref,
                  ucb_ref, kb_ref, vb_ref, q_ref, ya_ref, bias_ref, yn_ref, hn_ref, xn_ref, proj_ref,
                  *, tile, n_tiles, n_steps):
    s = pl.program_id(0)
    t = lax.rem(jnp.minimum(s, n_steps - 1), n_tiles)
    first = t == 0
    head = SUBLANES
    chunks = tile // CHUNK

    @pl.when(s == 0)
    def _():
        for cls in range(WIN_CHUNKS + 1):
            _store_paired_bias(
                lambda iv, r0, b, cls=cls: bias_ref.__setitem__((cls, iv, slice(r0, r0 + CHUNK)), b),
                _build_bias(ids_ref[cls], table_ref, sinks_ref), CHUNK)
        ucb_ref[...] = jnp.zeros(ucb_ref.shape, F32)
        kb_ref[...] = jnp.zeros(kb_ref.shape, BF16)
        vb_ref[:, :, 0:LANES] = jnp.zeros((N_VARIANTS, WINDOW + tile, LANES), BF16)
        for iv in range(N_VARIANTS):
            vb_ref[iv, :, LANES:] = _denominator_lanes(WINDOW + tile, WINDOW + tile, iv % 2)
        yn_ref[...] = jnp.zeros(yn_ref.shape, BF16)
        proj_ref[...] = _dot(_rms(xprev_ref[...], g_mix_ref[...]).astype(BF16), w_in_ref[...])

    groups = chunks // CHUNK_GROUP
    pieces = _split_columns(D_FF, MLP_PIECES)
    hidden = {}

    def mlp_up(idx):
        n0, n1 = pieces[idx]
        u = jnp.maximum(_dot(hn_ref[...], w_up_ref[:, n0:n1]), 0.0)
        hidden[idx] = (u * u).astype(BF16)

    def mlp_down(acc, idx):
        n0, n1 = pieces[idx]
        return acc + _dot(hidden.pop(idx), w_down_ref[n0:n1, :])

    def scores(i):
        cls = jnp.where(first, i, WIN_CHUNKS) if i < WIN_CHUNKS else WIN_CHUNKS
        rows = slice(i * CHUNK, (i + 1) * CHUNK)
        band = slice(i * CHUNK, i * CHUNK + BAND)
        out = []
        for j in range(N_KV_HEADS):
            c0 = j * GROUP * HEAD_DIM
            q_pairs = jnp.concatenate([q_ref[rows, c0:c0 + LANES], q_ref[rows, c0 + LANES:c0 + 2 * LANES]],
                                      axis=0)
            key_parts = [[kb_ref[2 * j + half, band, :], mk2_ref[2 * j + half]] for half in range(2)]
            out.append(_attend_scores(q_pairs, key_parts, [bias_ref[cls, 2 * j + half] for half in range(2)]))
        return out

    def values(i, probs):
        rows = slice(i * CHUNK, (i + 1) * CHUNK)
        band = slice(i * CHUNK, i * CHUNK + BAND)
        for j in range(N_KV_HEADS):
            c0 = j * GROUP * HEAD_DIM
            value_parts = [[vb_ref[2 * j + half, band, :], mv2_ref[2 * j + half]] for half in range(2)]
            o = _attend_values(probs[j], value_parts)
            ya_ref[rows, c0:c0 + LANES] = o[0:CHUNK]
            ya_ref[rows, c0 + LANES:c0 + 2 * LANES] = o[CHUNK:2 * CHUNK]

    h1 = xprev_ref[...] + _dot(yn_ref[...], w_out_ref[...])
    hn_ref[...] = _rms(h1, g_mlp_ref[...]).astype(BF16)

    xn_ref[...] = _rms(xnext_ref[...], g_mix_ref[...]).astype(BF16)

    uc = proj_ref[:, OFF_C:OFF_C + CONV_DIM] * proj_ref[:, OFF_U:OFF_U + CONV_DIM]
    k = proj_ref[:, OFF_K:OFF_K + KV_DIM]
    v = proj_ref[:, OFF_V:OFF_V + KV_DIM]

    ucb_ref[head - 2:head, :] = jnp.where(first, muc_ref[N_META - 2:N_META, :],
                                          ucb_ref[head + tile - 2:head + tile, :])
    zeros = jnp.zeros((N_VARIANTS, WINDOW, LANES), BF16)
    kb_ref[:, 0:WINDOW, :] = jnp.where(first, zeros, kb_ref[:, tile:tile + WINDOW, :])
    vb_ref[:, 0:WINDOW, 0:LANES] = jnp.where(first, zeros, vb_ref[:, tile:tile + WINDOW, 0:LANES])

    ucb_ref[head:head + tile, :] = uc
    for iv, var in enumerate(_lane_half_variants(k)):
        kb_ref[iv, WINDOW:WINDOW + tile, :] = var
    for iv, var in enumerate(_lane_half_variants(v)):
        vb_ref[iv, WINDOW:WINDOW + tile, 0:LANES] = var
    q_ref[...] = (proj_ref[:, OFF_Q:OFF_Q + Q_DIM] * (HEAD_DIM ** -0.5)).astype(BF16)

    pk_ref[...] = k[tile - WINDOW:tile]
    pv_ref[...] = v[tile - WINDOW:tile]
    pc_ref[...] = uc[tile - (CONV_W - 1):tile]

    conv = (conv_w_ref[0:1, :] * ucb_ref[head - 2:head - 2 + tile, :]
            + conv_w_ref[1:2, :] * ucb_ref[head - 1:head - 1 + tile, :]
            + conv_w_ref[2:3, :] * uc)
    yc = proj_ref[:, OFF_B:OFF_B + CONV_DIM] * conv
    yn_ref[:, 0:CONV_DIM] = _rms(yc, g_conv_ref[...]).astype(BF16)

    proj_ref[:, 0:OFF_Q] = _dot(xn_ref[...], w_in_ref[:, 0:OFF_Q])

    mlp_items = [("up", 0)]
    for idx in range(1, MLP_PIECES):
        mlp_items += [("up", idx), ("down", idx - 1)]
    mlp_items.append(("down", MLP_PIECES - 1))
    runs = _run_lengths(len(mlp_items), groups + 1)

    group_of = lambda g: range(g * CHUNK_GROUP, (g + 1) * CHUNK_GROUP)
    probs = [scores(i) for i in group_of(0)]
    h2 = h1
    items = iter(mlp_items)
    for g in range(groups + 1):
        for _ in range(runs[g]):
            kind, idx = next(items)
            if kind == "up":
                mlp_up(idx)
            else:
                h2 = mlp_down(h2, idx)
        if g < groups:
            for i, p in zip(group_of(g), probs):
                values(i, p)
        if g + 1 < groups:
            probs = [scores(i) for i in group_of(g + 1)]
    y_ref[...] = _rms(h2, g_final_ref[...])
    yn_ref[:, CONV_DIM:] = _rms(ya_ref[...], g_attn_ref[...]).astype(BF16)

    proj_ref[:, OFF_Q:] = _dot(xn_ref[...], w_in_ref[:, OFF_Q:])


def _resident(shape):
    return pl.BlockSpec(shape, lambda *_: (0,) * len(shape), pipeline_mode=pl.Buffered(1))


def _layer_call(x, g_mix, w_in, conv_w, mk2, mv2, muc, ids, table, sinks, g_conv, g_attn, w_out,
                g_mlp, w_up, w_down, g_final):
    n_batch, seq, _ = x.shape
    tile = LAYER_TILE
    n_tiles = seq // tile
    n_steps = n_batch * n_tiles
    assert seq % tile == 0 and tile % CHUNK == 0 and tile >= WINDOW
    n_keys = BAND + META_ROWS
    out_shape = (
        jax.ShapeDtypeStruct((n_batch, seq, D_MODEL), F32),
        jax.ShapeDtypeStruct((n_batch, WINDOW, KV_DIM), F32),
        jax.ShapeDtypeStruct((n_batch, WINDOW, KV_DIM), F32),
        jax.ShapeDtypeStruct((n_batch, CONV_W - 1, CONV_DIM), F32),
    )

    def tile_block(offset):
        def index_map(s):
            s = jnp.clip(s + offset, 0, n_steps - 1)
            return s // n_tiles, s % n_tiles, 0
        return index_map

    this_block, prev_block, next_block = tile_block(0), tile_block(-1), tile_block(1)
    per_batch = lambda shape: pl.BlockSpec((None,) + shape, lambda s: (this_block(s)[0], 0, 0))
    operands = (x, x, g_mix, w_in, conv_w, mk2, mv2, muc, ids, table, sinks, g_conv, g_attn, w_out,
                g_mlp, w_up, w_down, g_final)
    in_specs = [pl.BlockSpec((None, tile, D_MODEL), next_block), pl.BlockSpec((None, tile, D_MODEL), prev_block)]
    in_specs += [_smem() if a is table or a is sinks else _resident(a.shape) for a in operands[2:]]
    return pl.pallas_call(
        functools.partial(_layer_kernel, tile=tile, n_tiles=n_tiles, n_steps=n_steps),
        grid=(n_steps + 1,),
        in_specs=in_specs,
        out_specs=(
            pl.BlockSpec((None, tile, D_MODEL), prev_block),
            per_batch((WINDOW, KV_DIM)), per_batch((WINDOW, KV_DIM)), per_batch((CONV_W - 1, CONV_DIM)),
        ),
        out_shape=out_shape,
        scratch_shapes=[
            pltpu.VMEM((SUBLANES + tile, CONV_DIM), F32),
            pltpu.VMEM((N_VARIANTS, WINDOW + tile, LANES), BF16),
            pltpu.VMEM((N_VARIANTS, WINDOW + tile, 2 * LANES), BF16),
            pltpu.VMEM((tile, Q_DIM), BF16),
            pltpu.VMEM((tile, Q_DIM), F32),
            pltpu.VMEM((WIN_CHUNKS + 1, N_VARIANTS, 2 * CHUNK, n_keys), F32),
            pltpu.VMEM((tile, CONV_DIM + Q_DIM), BF16),
            pltpu.VMEM((tile, D_MODEL), BF16),
            pltpu.VMEM((tile, D_MODEL), BF16),
            pltpu.VMEM((tile, IN_DIM), F32),
        ],
        compiler_params=pltpu.CompilerParams(
            dimension_semantics=("arbitrary",), vmem_limit_bytes=VMEM_LIMIT_BYTES),
        name="prompt_layer",
    )(*operands)


def _mlp_kernel(h_ref, g_mlp_ref, w_up_ref, w_down_ref, g_final_ref, y_ref):
    h = h_ref[...]
    u = jnp.maximum(_dot(_rms(h, g_mlp_ref[...]).astype(BF16), w_up_ref[...]), 0.0)
    h2 = h + _dot((u * u).astype(BF16), w_down_ref[...])
    y_ref[...] = _rms(h2, g_final_ref[...])


def _mlp_call(h, g_mlp, w_up, w_down, g_final):
    n_tok = h.shape[0]
    tile = min(MLP_TILE, n_tok)
    assert n_tok % tile == 0
    return pl.pallas_call(
        _mlp_kernel,
        grid=(n_tok // tile,),
        in_specs=[
            pl.BlockSpec((tile, D_MODEL), lambda i: (i, 0)),
            _resident(g_mlp.shape), _resident(w_up.shape), _resident(w_down.shape), _resident(g_final.shape),
        ],
        out_specs=pl.BlockSpec((tile, D_MODEL), lambda i: (i, 0)),
        out_shape=jax.ShapeDtypeStruct((n_tok, D_MODEL), F32),
        compiler_params=pltpu.CompilerParams(
            dimension_semantics=("arbitrary",), vmem_limit_bytes=VMEM_LIMIT_BYTES),
        name="mlp",
    )(h, g_mlp, w_up, w_down, g_final)


def _t5_bucket(rp):
    nb = N_BUCKETS // 2
    max_exact = nb // 2
    ret = jnp.where(rp > 0, nb, 0)
    n = jnp.abs(rp)
    nf = jnp.maximum(n, 1).astype(F32)
    large = max_exact + (jnp.log(nf / max_exact) / math.log(MAX_DISTANCE / max_exact)
                         * (nb - max_exact)).astype(jnp.int32)
    large = jnp.minimum(large, nb - 1)
    return ret + jnp.where(n < max_exact, n, large)


def _bucket_ids(q_pos, k_pos, valid=None):
    ids = _t5_bucket(k_pos[None, :] - q_pos[:, None])
    if valid is not None:
        ids = jnp.where(valid[None, :], ids, MASKED_ID)
    n_q = q_pos.shape[0]
    tail = jnp.concatenate([jnp.full((n_q, 1), SINK_ID, jnp.int32),
                            jnp.full((n_q, TAIL_ROWS - 1), MASKED_ID, jnp.int32)], axis=1)
    return jnp.concatenate([ids, tail], axis=1)


def _prompt_bucket_ids():
    qi = jnp.arange(CHUNK, dtype=jnp.int32)
    r = jnp.arange(BAND, dtype=jnp.int32)
    pm = jnp.arange(N_META, dtype=jnp.int32)
    out = []
    for c in range(WIN_CHUNKS + 1):
        frame_k = c * CHUNK - WIN_CHUNKS * CHUNK + r
        k_pos = jnp.concatenate([N_META + frame_k, pm])
        valid = jnp.concatenate([frame_k >= 0, jnp.ones((N_META,), bool)])
        out.append(_bucket_ids(N_META + c * CHUNK + qi, k_pos, valid))
    return jnp.stack(out)


def _sample_bucket_ids(n_win, seq):
    s = jnp.arange(seq, dtype=jnp.int32)
    k_pos = jnp.concatenate([jnp.arange(N_META, dtype=jnp.int32),
                             N_META + PAST_LEN - n_win + jnp.arange(n_win, dtype=jnp.int32),
                             N_META + PAST_LEN + s])
    return _bucket_ids(N_META + PAST_LEN + s, k_pos)


def kernel(x_prompt, x_sample, cache_k, cache_v, cache_meta_k, cache_meta_v, state_conv, meta_tokens,
           norm_mix, w_in, conv_w, attn_sinks, rel_bias_table, norm_conv_out, norm_attn_out, w_out,
           norm_mlp, w_up, w_down, norm_final):
    n_batch, seq, _ = x_prompt.shape
    s_batch, s_seq, _ = x_sample.shape
    n_win = cache_k.shape[2]
    row = lambda a: a.reshape(1, -1)

    w_in_b = w_in[0].astype(BF16)
    w_out_b = w_out[0].astype(BF16)
    w_up_b = w_up[0].astype(BF16)
    w_down_b = w_down[0].astype(BF16)
    g_mix, g_conv, g_attn = row(norm_mix[0]), row(norm_conv_out[0]), row(norm_attn_out[0])
    g_mlp, g_final = row(norm_mlp[0]), row(norm_final)
    table = rel_bias_table.astype(F32)
    sinks = attn_sinks[0].astype(F32)

    h1_s, s_k, s_v, s_conv, mk2, mv2, mk, mv, muc = _aux_call(
        x_sample.reshape(s_batch * s_seq, D_MODEL), meta_tokens, g_mix, w_in_b, conv_w[0], state_conv[0],
        cache_k[0].reshape(s_batch, n_win, KV_DIM), cache_v[0].reshape(s_batch, n_win, KV_DIM),
        cache_meta_k[0].reshape(s_batch, N_META, KV_DIM), cache_meta_v[0].reshape(s_batch, N_META, KV_DIM),
        _sample_bucket_ids(n_win, s_seq), table, sinks, g_conv, g_attn, w_out_b, n_batch=s_batch, seq=s_seq)

    y_prompt, p_k, p_v, p_conv = _layer_call(
        x_prompt, g_mix, w_in_b, conv_w[0], mk2, mv2, muc, _prompt_bucket_ids(), table, sinks,
        g_conv, g_attn, w_out_b, g_mlp, w_up_b, w_down_b, g_final)
    y_sample = _mlp_call(h1_s, g_mlp, w_up_b, w_down_b, g_final)

    kv_shape = lambda a, n, length: a.reshape(1, n, length, N_KV_HEADS, HEAD_DIM)
    meta_shape = (1, n_batch, N_META, N_KV_HEADS, HEAD_DIM)
    return (
        y_prompt,
        y_sample.reshape(s_batch, s_seq, D_MODEL),
        kv_shape(p_k, n_batch, WINDOW), kv_shape(p_v, n_batch, WINDOW),
        jnp.broadcast_to(mk.reshape(1, 1, N_META, N_KV_HEADS, HEAD_DIM), meta_shape),
        jnp.broadcast_to(mv.reshape(1, 1, N_META, N_KV_HEADS, HEAD_DIM), meta_shape),
        p_conv[None],
        kv_shape(s_k, s_batch, s_seq), kv_shape(s_v, s_batch, s_seq),
        s_conv[None],
    )
```

```python
import functools
import math

import jax
import jax.numpy as jnp
from jax import lax
from jax.experimental import pallas as pl
from jax.experimental.pallas import tpu as pltpu

D_MODEL = 1024
CHUNK = 64
N_META = 16
CONV_DIM = 512
CONV_W = 3
N_HEADS = 8
N_KV_HEADS = 2
HEAD_DIM = 64
GROUP = N_HEADS // N_KV_HEADS
Q_DIM = N_HEADS * HEAD_DIM
KV_DIM = N_KV_HEADS * HEAD_DIM
IN_DIM = 3 * CONV_DIM + Q_DIM + 2 * KV_DIM
WINDOW = 128
WIN_CHUNKS = WINDOW // CHUNK
BAND = (WIN_CHUNKS + 1) * CHUNK
N_BUCKETS = 32
MAX_DISTANCE = 128
D_FF = 4 * D_MODEL
EPS = 1e-6
PAST_LEN = 4096

OFF_B, OFF_C, OFF_U = 0, CONV_DIM, 2 * CONV_DIM
OFF_Q = 3 * CONV_DIM
OFF_K = OFF_Q + Q_DIM
OFF_V = OFF_K + KV_DIM

LANES = 128
SUBLANES = 8
BF16_ROWS = 16
MXU_TILE = 256
VMEM_LIMIT_BYTES = 56 * 1024 * 1024
MASK_VALUE = -1e30

LAYER_TILE = 256
CHUNK_GROUP = 2
MLP_PIECES = 8
MLP_TILE = 512
TAIL_ROWS = BF16_ROWS
META_ROWS = N_META + TAIL_ROWS
MASKED_ID = -1
SINK_ID = -2
HEAD_ORDER = (0, 2, 1, 3)

BF16 = jnp.bfloat16
F32 = jnp.float32


def _rms(x, g):
    ms = jnp.mean(x * x, axis=-1, keepdims=True)
    return x * lax.rsqrt(ms + EPS) * g


def _dot(a, b):
    return jnp.dot(a, b, preferred_element_type=F32)


def _dot_nt(a, b):
    return lax.dot_general(a, b, (((1,), (1,)), ((), ())), preferred_element_type=F32)


def _swap_lane_halves(x):
    pairs = [pltpu.roll(x[:, c:c + LANES], HEAD_DIM, axis=1) for c in range(0, x.shape[1], LANES)]
    return jnp.concatenate(pairs, axis=1)


def _head_layouts(kv):
    lane = lax.broadcasted_iota(jnp.int32, kv.shape, 1)
    low = lane < HEAD_DIM
    zero = jnp.zeros_like(kv)
    return (jnp.where(low, kv, zero).astype(BF16),
            jnp.where(low, pltpu.roll(kv, HEAD_DIM, axis=1), zero).astype(BF16))


def _denominator_lanes(n_rows, n_counted):
    row = lax.broadcasted_iota(jnp.int32, (n_rows, LANES), 0)
    return jnp.where(row < n_counted, 1.0, 0.0).astype(BF16)


def _build_bias(ids, table_ref, sinks_ref):
    out = []
    for h in range(N_HEADS):
        acc = jnp.full(ids.shape, MASK_VALUE, F32)
        for bucket in range(N_BUCKETS):
            acc = jnp.where(ids == bucket, table_ref[bucket, h], acc)
        out.append(jnp.where(ids == SINK_ID, sinks_ref[h], acc))
    return out


def _store_stacked_bias(bias_ref, prefix, per_head, q_rows):
    for h, b in enumerate(per_head):
        j, g = divmod(h, GROUP)
        r0 = HEAD_ORDER.index(g) * q_rows
        bias_ref[prefix + (j, slice(r0, r0 + q_rows))] = b


def _run_lengths(n_items, n_runs):
    base, extra = divmod(n_items, n_runs)
    return [base + (1 if i >= n_runs - extra else 0) for i in range(n_runs)]


def _split_columns(width, n_parts):
    size, rem = divmod(width, n_parts)
    assert rem == 0 and size % MXU_TILE == 0
    return [(i * size, (i + 1) * size) for i in range(n_parts)]


def _stack_heads(q_ref, qsw_ref, rows, j):
    c0 = j * GROUP * HEAD_DIM
    return jnp.concatenate([q_ref[rows, c0:c0 + LANES], q_ref[rows, c0 + LANES:c0 + 2 * LANES],
                            qsw_ref[rows, c0:c0 + LANES], qsw_ref[rows, c0 + LANES:c0 + 2 * LANES]], axis=0)


def _attend_scores(q_heads, key_parts, bias):
    s = _dot_nt(q_heads, jnp.concatenate(key_parts, axis=0)) + bias
    return jnp.exp(s - jnp.max(s, axis=-1, keepdims=True)).astype(BF16)


def _attend_values(probs, value_parts, q_rows):
    o = _dot(probs, jnp.concatenate(value_parts, axis=0))
    o = o[:, :LANES] / o[:, LANES:]
    even, odd = o[:2 * q_rows], pltpu.roll(o[2 * q_rows:], HEAD_DIM, axis=1)
    pairs = even + odd
    return pairs[:q_rows], pairs[q_rows:]


def _aux_kernel(xs_ref, meta_ref, g_mix_ref, w_in_ref, conv_w_ref, state_ref, ck_ref, cv_ref,
                cmk_ref, cmv_ref, ids_ref, table_ref, sinks_ref, g_conv_ref, g_attn_ref, w_out_ref,
                h1_ref, sk_ref, sv_ref, sconv_ref, mk2_ref, mv2_ref, mk_ref, mv_ref, muc_ref,
                ucb_ref, q_ref, qsw_ref, ya_ref, yc_ref, bias_ref, *, n_batch, seq, n_keys):
    g_mix = g_mix_ref[...]
    w_in = w_in_ref[...]
    tail = jnp.zeros((TAIL_ROWS, LANES), F32)

    mproj = _dot(_rms(meta_ref[...], g_mix).astype(BF16), w_in)
    mk = mproj[:, OFF_K:OFF_K + KV_DIM]
    mv = mproj[:, OFF_V:OFF_V + KV_DIM]
    mk_ref[...] = mk
    mv_ref[...] = mv
    muc_ref[...] = mproj[:, OFF_C:OFF_C + CONV_DIM] * mproj[:, OFF_U:OFF_U + CONV_DIM]
    for j, layout in enumerate(_head_layouts(jnp.concatenate([mk, tail], axis=0))):
        mk2_ref[j] = layout
    for j, layout in enumerate(_head_layouts(jnp.concatenate([mv, tail], axis=0))):
        mv2_ref[j] = jnp.concatenate([layout, _denominator_lanes(META_ROWS, N_META + 1)], axis=1)

    _store_stacked_bias(bias_ref, (), _build_bias(ids_ref[...], table_ref, sinks_ref), seq)

    xs = xs_ref[...]
    proj = _dot(_rms(xs, g_mix).astype(BF16), w_in)
    uc = proj[:, OFF_C:OFF_C + CONV_DIM] * proj[:, OFF_U:OFF_U + CONV_DIM]
    sk_ref[...] = proj[:, OFF_K:OFF_K + KV_DIM]
    sv_ref[...] = proj[:, OFF_V:OFF_V + KV_DIM]
    q = proj[:, OFF_Q:OFF_Q + Q_DIM] * (HEAD_DIM ** -0.5)
    q_ref[...] = q.astype(BF16)
    qsw_ref[...] = _swap_lane_halves(q).astype(BF16)
    yc_ref[...] = proj[:, OFF_B:OFF_B + CONV_DIM]
    w0 = conv_w_ref[0:1, :]
    w1 = conv_w_ref[1:2, :]
    w2 = conv_w_ref[2:3, :]
    head = SUBLANES
    for b in range(n_batch):
        ucb_ref[b, head - 2:head, :] = state_ref[b]
        ucb_ref[b, head:head + seq, :] = uc[b * seq:(b + 1) * seq]
        sconv_ref[b] = uc[(b + 1) * seq - 2:(b + 1) * seq]
    den_lanes = _denominator_lanes(n_keys + TAIL_ROWS, n_keys + 1)

    def batch_body(b, carry):
        r0 = pl.multiple_of(b * seq, seq)
        rows = pl.ds(r0, seq)
        conv = (w0 * ucb_ref[b, head - 2:head - 2 + seq, :] + w1 * ucb_ref[b, head - 1:head - 1 + seq, :]
                + w2 * ucb_ref[b, head:head + seq, :])
        yc_ref[rows, :] = yc_ref[rows, :] * conv
        keys = _head_layouts(jnp.concatenate([cmk_ref[b], ck_ref[b], sk_ref[rows, :], tail], axis=0))
        vals = _head_layouts(jnp.concatenate([cmv_ref[b], cv_ref[b], sv_ref[rows, :], tail], axis=0))
        for j in range(N_KV_HEADS):
            c0 = j * GROUP * HEAD_DIM
            probs = _attend_scores(_stack_heads(q_ref, qsw_ref, rows, j), [keys[j]], bias_ref[j])
            pair0, pair1 = _attend_values(probs, [jnp.concatenate([vals[j], den_lanes], axis=1)], seq)
            ya_ref[rows, c0:c0 + LANES] = pair0
            ya_ref[rows, c0 + LANES:c0 + 2 * LANES] = pair1
        return carry

    lax.fori_loop(0, n_batch, batch_body, 0)

    yc_n = _rms(yc_ref[...], g_conv_ref[...]).astype(BF16)
    ya_n = _rms(ya_ref[...], g_attn_ref[...]).astype(BF16)
    h1_ref[...] = xs + _dot(yc_n, w_out_ref[0:CONV_DIM, :]) + _dot(ya_n, w_out_ref[CONV_DIM:, :])


def _smem():
    return pl.BlockSpec(memory_space=pltpu.SMEM)


def _aux_call(xs, meta, g_mix, w_in, conv_w, state, ck, cv, cmk, cmv, ids, table, sinks, g_conv, g_attn, w_out,
              *, n_batch, seq):
    n_tok = n_batch * seq
    n_keys = ids.shape[1] - TAIL_ROWS
    out_shape = (
        jax.ShapeDtypeStruct((n_tok, D_MODEL), F32),
        jax.ShapeDtypeStruct((n_tok, KV_DIM), F32),
        jax.ShapeDtypeStruct((n_tok, KV_DIM), F32),
        jax.ShapeDtypeStruct((n_batch, CONV_W - 1, CONV_DIM), F32),
        jax.ShapeDtypeStruct((N_KV_HEADS, META_ROWS, LANES), BF16),
        jax.ShapeDtypeStruct((N_KV_HEADS, META_ROWS, 2 * LANES), BF16),
        jax.ShapeDtypeStruct((N_META, KV_DIM), F32),
        jax.ShapeDtypeStruct((N_META, KV_DIM), F32),
        jax.ShapeDtypeStruct((N_META, CONV_DIM), F32),
    )
    vmem = pl.BlockSpec(memory_space=pltpu.VMEM)
    return pl.pallas_call(
        functools.partial(_aux_kernel, n_batch=n_batch, seq=seq, n_keys=n_keys),
        in_specs=[vmem] * 11 + [_smem(), _smem()] + [vmem] * 3,
        out_shape=out_shape,
        scratch_shapes=[
            pltpu.VMEM((n_batch, SUBLANES + seq, CONV_DIM), F32),
            pltpu.VMEM((n_tok, Q_DIM), BF16),
            pltpu.VMEM((n_tok, Q_DIM), BF16),
            pltpu.VMEM((n_tok, Q_DIM), F32),
            pltpu.VMEM((n_tok, CONV_DIM), F32),
            pltpu.VMEM((N_KV_HEADS, GROUP * seq, n_keys + TAIL_ROWS), F32),
        ],
        compiler_params=pltpu.CompilerParams(vmem_limit_bytes=VMEM_LIMIT_BYTES),
        name="aux_sample_mixer",
    )(xs, meta, g_mix, w_in, conv_w, state, ck, cv, cmk, cmv, ids, table, sinks, g_conv, g_attn, w_out)


def _layer_kernel(xnext_ref, xprev_ref, g_mix_ref, w_in_ref, conv_w_ref, mk2_ref, mv2_ref, muc_ref, ids_ref,
                  table_ref, sinks_ref, g_conv_ref, g_attn_ref, w_out_ref, g_mlp_ref, w_up_ref, w_down_ref,
                  g_final_ref,
                  y_ref, pk_ref, pv_ref, pc_ref,
                  ucb_ref, kb_ref, vb_ref, q_ref, qsw_ref, ya_ref, bias_ref, yn_ref, hn_ref, xn_ref, proj_ref,
                  *, tile, n_tiles, n_steps):
    s = pl.program_id(0)
    t = lax.rem(jnp.minimum(s, n_steps - 1), n_tiles)
    first = t == 0
    head = SUBLANES
    chunks = tile // CHUNK

    @pl.when(s == 0)
    def _():
        for cls in range(WIN_CHUNKS + 1):
            _store_stacked_bias(bias_ref, (cls,), _build_bias(ids_ref[cls], table_ref, sinks_ref), CHUNK)
        ucb_ref[...] = jnp.zeros(ucb_ref.shape, F32)
        kb_ref[...] = jnp.zeros(kb_ref.shape, BF16)
        ones = _denominator_lanes(WINDOW + tile, WINDOW + tile)
        for j in range(N_KV_HEADS):
            vb_ref[j] = jnp.concatenate([jnp.zeros_like(ones), ones], axis=1)
        yn_ref[...] = jnp.zeros(yn_ref.shape, BF16)
        proj_ref[...] = _dot(_rms(xprev_ref[...], g_mix_ref[...]).astype(BF16), w_in_ref[...])

    groups = chunks // CHUNK_GROUP
    pieces = _split_columns(D_FF, MLP_PIECES)
    hidden = {}

    def mlp_up(idx):
        n0, n1 = pieces[idx]
        u = jnp.maximum(_dot(hn_ref[...], w_up_ref[:, n0:n1]), 0.0)
        hidden[idx] = (u * u).astype(BF16)

    def mlp_down(acc, idx):
        n0, n1 = pieces[idx]
        return acc + _dot(hidden.pop(idx), w_down_ref[n0:n1, :])

    def scores(i):
        cls = jnp.where(first, i, WIN_CHUNKS) if i < WIN_CHUNKS else WIN_CHUNKS
        rows = slice(i * CHUNK, (i + 1) * CHUNK)
        band = slice(i * CHUNK, i * CHUNK + BAND)
        return [_attend_scores(_stack_heads(q_ref, qsw_ref, rows, j), [kb_ref[j, band, :], mk2_ref[j]],
                               bias_ref[cls, j]) for j in range(N_KV_HEADS)]

    def values(i, probs):
        rows = slice(i * CHUNK, (i + 1) * CHUNK)
        band = slice(i * CHUNK, i * CHUNK + BAND)
        for j in range(N_KV_HEADS):
            c0 = j * GROUP * HEAD_DIM
            pair0, pair1 = _attend_values(probs[j], [vb_ref[j, band, :], mv2_ref[j]], CHUNK)
            ya_ref[rows, c0:c0 + LANES] = pair0
            ya_ref[rows, c0 + LANES:c0 + 2 * LANES] = pair1

    h1 = xprev_ref[...] + _dot(yn_ref[...], w_out_ref[...])
    hn_ref[...] = _rms(h1, g_mlp_ref[...]).astype(BF16)

    xn_ref[...] = _rms(xnext_ref[...], g_mix_ref[...]).astype(BF16)

    uc = proj_ref[:, OFF_C:OFF_C + CONV_DIM] * proj_ref[:, OFF_U:OFF_U + CONV_DIM]
    k = proj_ref[:, OFF_K:OFF_K + KV_DIM]
    v = proj_ref[:, OFF_V:OFF_V + KV_DIM]

    ucb_ref[head - 2:head, :] = jnp.where(first, muc_ref[N_META - 2:N_META, :],
                                          ucb_ref[head + tile - 2:head + tile, :])
    zeros = jnp.zeros((N_KV_HEADS, WINDOW, LANES), BF16)
    kb_ref[:, 0:WINDOW, :] = jnp.where(first, zeros, kb_ref[:, tile:tile + WINDOW, :])
    vb_ref[:, 0:WINDOW, 0:LANES] = jnp.where(first, zeros, vb_ref[:, tile:tile + WINDOW, 0:LANES])

    ucb_ref[head:head + tile, :] = uc
    for j, layout in enumerate(_head_layouts(k)):
        kb_ref[j, WINDOW:WINDOW + tile, :] = layout
    for j, layout in enumerate(_head_layouts(v)):
        vb_ref[j, WINDOW:WINDOW + tile, 0:LANES] = layout
    q = proj_ref[:, OFF_Q:OFF_Q + Q_DIM] * (HEAD_DIM ** -0.5)
    q_ref[...] = q.astype(BF16)
    qsw_ref[...] = _swap_lane_halves(q).astype(BF16)

    pk_ref[...] = k[tile - WINDOW:tile]
    pv_ref[...] = v[tile - WINDOW:tile]
    pc_ref[...] = uc[tile - (CONV_W - 1):tile]

    conv = (conv_w_ref[0:1, :] * ucb_ref[head - 2:head - 2 + tile, :]
            + conv_w_ref[1:2, :] * ucb_ref[head - 1:head - 1 + tile, :]
            + conv_w_ref[2:3, :] * uc)
    yc = proj_ref[:, OFF_B:OFF_B + CONV_DIM] * conv
    yn_ref[:, 0:CONV_DIM] = _rms(yc, g_conv_ref[...]).astype(BF16)

    proj_ref[:, 0:OFF_Q] = _dot(xn_ref[...], w_in_ref[:, 0:OFF_Q])

    mlp_items = [("up", 0)]
    for idx in range(1, MLP_PIECES):
        mlp_items += [("up", idx), ("down", idx - 1)]
    mlp_items.append(("down", MLP_PIECES - 1))
    runs = _run_lengths(len(mlp_items), groups + 1)

    group_of = lambda g: range(g * CHUNK_GROUP, (g + 1) * CHUNK_GROUP)
    probs = [scores(i) for i in group_of(0)]
    h2 = h1
    items = iter(mlp_items)
    for g in range(groups + 1):
        for _ in range(runs[g]):
            kind, idx = next(items)
            if kind == "up":
                mlp_up(idx)
            else:
                h2 = mlp_down(h2, idx)
        if g < groups:
            for i, p in zip(group_of(g), probs):
                values(i, p)
        if g + 1 < groups:
            probs = [scores(i) for i in group_of(g + 1)]
    y_ref[...] = _rms(h2, g_final_ref[...])
    yn_ref[:, CONV_DIM:] = _rms(ya_ref[...], g_attn_ref[...]).astype(BF16)

    proj_ref[:, OFF_Q:] = _dot(xn_ref[...], w_in_ref[:, OFF_Q:])


def _resident(shape):
    return pl.BlockSpec(shape, lambda *_: (0,) * len(shape), pipeline_mode=pl.Buffered(1))


def _layer_call(x, g_mix, w_in, conv_w, mk2, mv2, muc, ids, table, sinks, g_conv, g_attn, w_out,
                g_mlp, w_up, w_down, g_final):
    n_batch, seq, _ = x.shape
    tile = LAYER_TILE
    n_tiles = seq // tile
    n_steps = n_batch * n_tiles
    assert seq % tile == 0 and tile % (CHUNK * CHUNK_GROUP) == 0 and tile >= WINDOW
    n_keys = BAND + META_ROWS
    out_shape = (
        jax.ShapeDtypeStruct((n_batch, seq, D_MODEL), F32),
        jax.ShapeDtypeStruct((n_batch, WINDOW, KV_DIM), F32),
        jax.ShapeDtypeStruct((n_batch, WINDOW, KV_DIM), F32),
        jax.ShapeDtypeStruct((n_batch, CONV_W - 1, CONV_DIM), F32),
    )

    def tile_block(offset):
        def index_map(s):
            s = jnp.clip(s + offset, 0, n_steps - 1)
            return s // n_tiles, s % n_tiles, 0
        return index_map

    this_block, prev_block, next_block = tile_block(0), tile_block(-1), tile_block(1)
    per_batch = lambda shape: pl.BlockSpec((None,) + shape, lambda s: (this_block(s)[0], 0, 0))
    operands = (x, x, g_mix, w_in, conv_w, mk2, mv2, muc, ids, table, sinks, g_conv, g_attn, w_out,
                g_mlp, w_up, w_down, g_final)
    in_specs = [pl.BlockSpec((None, tile, D_MODEL), next_block), pl.BlockSpec((None, tile, D_MODEL), prev_block)]
    in_specs += [_smem() if a is table or a is sinks else _resident(a.shape) for a in operands[2:]]
    return pl.pallas_call(
        functools.partial(_layer_kernel, tile=tile, n_tiles=n_tiles, n_steps=n_steps),
        grid=(n_steps + 1,),
        in_specs=in_specs,
        out_specs=(
            pl.BlockSpec((None, tile, D_MODEL), prev_block),
            per_batch((WINDOW, KV_DIM)), per_batch((WINDOW, KV_DIM)), per_batch((CONV_W - 1, CONV_DIM)),
        ),
        out_shape=out_shape,
        scratch_shapes=[
            pltpu.VMEM((SUBLANES + tile, CONV_DIM), F32),
            pltpu.VMEM((N_KV_HEADS, WINDOW + tile, LANES), BF16),
            pltpu.VMEM((N_KV_HEADS, WINDOW + tile, 2 * LANES), BF16),
            pltpu.VMEM((tile, Q_DIM), BF16),
            pltpu.VMEM((tile, Q_DIM), BF16),
            pltpu.VMEM((tile, Q_DIM), F32),
            pltpu.VMEM((WIN_CHUNKS + 1, N_KV_HEADS, GROUP * CHUNK, n_keys), F32),
            pltpu.VMEM((tile, CONV_DIM + Q_DIM), BF16),
            pltpu.VMEM((tile, D_MODEL), BF16),
            pltpu.VMEM((tile, D_MODEL), BF16),
            pltpu.VMEM((tile, IN_DIM), F32),
        ],
        compiler_params=pltpu.CompilerParams(
            dimension_semantics=("arbitrary",), vmem_limit_bytes=VMEM_LIMIT_BYTES),
        name="prompt_layer",
    )(*operands)


def _mlp_kernel(h_ref, g_mlp_ref, w_up_ref, w_down_ref, g_final_ref, y_ref):
    h = h_ref[...]
    u = jnp.maximum(_dot(_rms(h, g_mlp_ref[...]).astype(BF16), w_up_ref[...]), 0.0)
    h2 = h + _dot((u * u).astype(BF16), w_down_ref[...])
    y_ref[...] = _rms(h2, g_final_ref[...])


def _mlp_call(h, g_mlp, w_up, w_down, g_final):
    n_tok = h.shape[0]
    tile = min(MLP_TILE, n_tok)
    assert n_tok % tile == 0
    return pl.pallas_call(
        _mlp_kernel,
        grid=(n_tok // tile,),
        in_specs=[
            pl.BlockSpec((tile, D_MODEL), lambda i: (i, 0)),
            _resident(g_mlp.shape), _resident(w_up.shape), _resident(w_down.shape), _resident(g_final.shape),
        ],
        out_specs=pl.BlockSpec((tile, D_MODEL), lambda i: (i, 0)),
        out_shape=jax.ShapeDtypeStruct((n_tok, D_MODEL), F32),
        compiler_params=pltpu.CompilerParams(
            dimension_semantics=("arbitrary",), vmem_limit_bytes=VMEM_LIMIT_BYTES),
        name="mlp",
    )(h, g_mlp, w_up, w_down, g_final)


def _t5_bucket(rp):
    nb = N_BUCKETS // 2
    max_exact = nb // 2
    ret = jnp.where(rp > 0, nb, 0)
    n = jnp.abs(rp)
    nf = jnp.maximum(n, 1).astype(F32)
    large = max_exact + (jnp.log(nf / max_exact) / math.log(MAX_DISTANCE / max_exact)
                         * (nb - max_exact)).astype(jnp.int32)
    large = jnp.minimum(large, nb - 1)
    return ret + jnp.where(n < max_exact, n, large)


def _bucket_ids(q_pos, k_pos, valid=None):
    ids = _t5_bucket(k_pos[None, :] - q_pos[:, None])
    if valid is not None:
        ids = jnp.where(valid[None, :], ids, MASKED_ID)
    n_q = q_pos.shape[0]
    tail = jnp.concatenate([jnp.full((n_q, 1), SINK_ID, jnp.int32),
                            jnp.full((n_q, TAIL_ROWS - 1), MASKED_ID, jnp.int32)], axis=1)
    return jnp.concatenate([ids, tail], axis=1)


def _prompt_bucket_ids():
    qi = jnp.arange(CHUNK, dtype=jnp.int32)
    r = jnp.arange(BAND, dtype=jnp.int32)
    pm = jnp.arange(N_META, dtype=jnp.int32)
    out = []
    for c in range(WIN_CHUNKS + 1):
        frame_k = c * CHUNK - WIN_CHUNKS * CHUNK + r
        k_pos = jnp.concatenate([N_META + frame_k, pm])
        valid = jnp.concatenate([frame_k >= 0, jnp.ones((N_META,), bool)])
        out.append(_bucket_ids(N_META + c * CHUNK + qi, k_pos, valid))
    return jnp.stack(out)


def _sample_bucket_ids(n_win, seq):
    s = jnp.arange(seq, dtype=jnp.int32)
    k_pos = jnp.concatenate([jnp.arange(N_META, dtype=jnp.int32),
                             N_META + PAST_LEN - n_win + jnp.arange(n_win, dtype=jnp.int32),
                             N_META + PAST_LEN + s])
    return _bucket_ids(N_META + PAST_LEN + s, k_pos)


def kernel(x_prompt, x_sample, cache_k, cache_v, cache_meta_k, cache_meta_v, state_conv, meta_tokens,
           norm_mix, w_in, conv_w, attn_sinks, rel_bias_table, norm_conv_out, norm_attn_out, w_out,
           norm_mlp, w_up, w_down, norm_final):
    n_batch, seq, _ = x_prompt.shape
    s_batch, s_seq, _ = x_sample.shape
    n_win = cache_k.shape[2]
    row = lambda a: a.reshape(1, -1)

    w_in_b = w_in[0].astype(BF16)
    w_out_b = w_out[0].astype(BF16)
    w_up_b = w_up[0].astype(BF16)
    w_down_b = w_down[0].astype(BF16)
    g_mix, g_conv, g_attn = row(norm_mix[0]), row(norm_conv_out[0]), row(norm_attn_out[0])
    g_mlp, g_final = row(norm_mlp[0]), row(norm_final)
    table = rel_bias_table.astype(F32)
    sinks = attn_sinks[0].astype(F32)

    h1_s, s_k, s_v, s_conv, mk2, mv2, mk, mv, muc = _aux_call(
        x_sample.reshape(s_batch * s_seq, D_MODEL), meta_tokens, g_mix, w_in_b, conv_w[0], state_conv[0],
        cache_k[0].reshape(s_batch, n_win, KV_DIM), cache_v[0].reshape(s_batch, n_win, KV_DIM),
        cache_meta_k[0].reshape(s_batch, N_META, KV_DIM), cache_meta_v[0].reshape(s_batch, N_META, KV_DIM),
        _sample_bucket_ids(n_win, s_seq), table, sinks, g_conv, g_attn, w_out_b, n_batch=s_batch, seq=s_seq)

    y_prompt, p_k, p_v, p_conv = _layer_call(
        x_prompt, g_mix, w_in_b, conv_w[0], mk2, mv2, muc, _prompt_bucket_ids(), table, sinks,
        g_conv, g_attn, w_out_b, g_mlp, w_up_b, w_down_b, g_final)
    y_sample = _mlp_call(h1_s, g_mlp, w_up_b, w_down_b, g_final)

    kv_shape = lambda a, n, length: a.reshape(1, n, length, N_KV_HEADS, HEAD_DIM)
    meta_shape = (1, n_batch, N_META, N_KV_HEADS, HEAD_DIM)
    return (
        y_prompt,
        y_sample.reshape(s_batch, s_seq, D_MODEL),
        kv_shape(p_k, n_batch, WINDOW), kv_shape(p_v, n_batch, WINDOW),
        jnp.broadcast_to(mk.reshape(1, 1, N_META, N_KV_HEADS, HEAD_DIM), meta_shape),
        jnp.broadcast_to(mv.reshape(1, 1, N_META, N_KV_HEADS, HEAD_DIM), meta_shape),
        p_conv[None],
        kv_shape(s_k, s_batch, s_seq), kv_shape(s_v, s_batch, s_seq),
        s_conv[None],
    )
```

```python
import functools
import math

import jax
import jax.numpy as jnp
from jax import lax
from jax.experimental import pallas as pl
from jax.experimental.pallas import tpu as pltpu

D_MODEL = 1024
CHUNK = 64
N_META = 16
CONV_DIM = 512
CONV_W = 3
N_HEADS = 8
N_KV_HEADS = 2
HEAD_DIM = 64
GROUP = N_HEADS // N_KV_HEADS
Q_DIM = N_HEADS * HEAD_DIM
KV_DIM = N_KV_HEADS * HEAD_DIM
IN_DIM = 3 * CONV_DIM + Q_DIM + 2 * KV_DIM
WINDOW = 128
WIN_CHUNKS = WINDOW // CHUNK
BAND = (WIN_CHUNKS + 1) * CHUNK
N_BUCKETS = 32
MAX_DISTANCE = 128
D_FF = 4 * D_MODEL
EPS = 1e-6
PAST_LEN = 4096

OFF_B, OFF_C, OFF_U = 0, CONV_DIM, 2 * CONV_DIM
OFF_Q = 3 * CONV_DIM
OFF_K = OFF_Q + Q_DIM
OFF_V = OFF_K + KV_DIM

LANES = 128
SUBLANES = 8
BF16_ROWS = 16
MXU_TILE = 256
VMEM_LIMIT_BYTES = 56 * 1024 * 1024
MASK_VALUE = -1e30

LAYER_TILE = 256
TILES_PER_STEP = 2
CHUNK_GROUP = 2
MLP_PIECES = 8
MLP_TILE = 512
TAIL_ROWS = BF16_ROWS
META_ROWS = N_META + TAIL_ROWS
MASKED_ID = -1
SINK_ID = -2
HEAD_ORDER = (0, 2, 1, 3)

BF16 = jnp.bfloat16
F32 = jnp.float32


def _rms(x, g):
    ms = jnp.mean(x * x, axis=-1, keepdims=True)
    return x * lax.rsqrt(ms + EPS) * g


def _dot(a, b):
    return jnp.dot(a, b, preferred_element_type=F32)


def _dot_nt(a, b):
    return lax.dot_general(a, b, (((1,), (1,)), ((), ())), preferred_element_type=F32)


def _swap_lane_halves(x):
    pairs = [pltpu.roll(x[:, c:c + LANES], HEAD_DIM, axis=1) for c in range(0, x.shape[1], LANES)]
    return jnp.concatenate(pairs, axis=1)


def _head_layouts(kv):
    lane = lax.broadcasted_iota(jnp.int32, kv.shape, 1)
    low = lane < HEAD_DIM
    zero = jnp.zeros_like(kv)
    return (jnp.where(low, kv, zero).astype(BF16),
            jnp.where(low, pltpu.roll(kv, HEAD_DIM, axis=1), zero).astype(BF16))


def _denominator_lanes(n_rows, n_counted):
    row = lax.broadcasted_iota(jnp.int32, (n_rows, LANES), 0)
    return jnp.where(row < n_counted, 1.0, 0.0).astype(BF16)


def _build_bias(ids, table_ref, sinks_ref):
    out = []
    for h in range(N_HEADS):
        acc = jnp.full(ids.shape, MASK_VALUE, F32)
        for bucket in range(N_BUCKETS):
            acc = jnp.where(ids == bucket, table_ref[bucket, h], acc)
        out.append(jnp.where(ids == SINK_ID, sinks_ref[h], acc))
    return out


def _store_stacked_bias(bias_ref, prefix, per_head, q_rows):
    for h, b in enumerate(per_head):
        j, g = divmod(h, GROUP)
        r0 = HEAD_ORDER.index(g) * q_rows
        bias_ref[prefix + (j, slice(r0, r0 + q_rows))] = b


def _run_lengths(n_items, n_runs):
    base, extra = divmod(n_items, n_runs)
    return [base + (1 if i >= n_runs - extra else 0) for i in range(n_runs)]


def _split_columns(width, n_parts):
    size, rem = divmod(width, n_parts)
    assert rem == 0 and size % MXU_TILE == 0
    return [(i * size, (i + 1) * size) for i in range(n_parts)]


def _stack_heads(q_ref, qsw_ref, rows, j):
    c0 = j * GROUP * HEAD_DIM
    return jnp.concatenate([q_ref[rows, c0:c0 + LANES], q_ref[rows, c0 + LANES:c0 + 2 * LANES],
                            qsw_ref[rows, c0:c0 + LANES], qsw_ref[rows, c0 + LANES:c0 + 2 * LANES]], axis=0)


def _attend_scores(q_heads, key_parts, bias):
    s = _dot_nt(q_heads, jnp.concatenate(key_parts, axis=0)) + bias
    return jnp.exp(s - jnp.max(s, axis=-1, keepdims=True)).astype(BF16)


def _attend_values(probs, value_parts, q_rows):
    o = _dot(probs, jnp.concatenate(value_parts, axis=0))
    o = o[:, :LANES] / o[:, LANES:]
    even, odd = o[:2 * q_rows], pltpu.roll(o[2 * q_rows:], HEAD_DIM, axis=1)
    pairs = even + odd
    return pairs[:q_rows], pairs[q_rows:]


def _aux_kernel(xs_ref, meta_ref, g_mix_ref, w_in_ref, conv_w_ref, state_ref, ck_ref, cv_ref,
                cmk_ref, cmv_ref, ids_ref, table_ref, sinks_ref, g_conv_ref, g_attn_ref, w_out_ref,
                h1_ref, sk_ref, sv_ref, sconv_ref, mk2_ref, mv2_ref, mk_ref, mv_ref, muc_ref,
                ucb_ref, q_ref, qsw_ref, ya_ref, yc_ref, bias_ref, *, n_batch, seq, n_keys):
    g_mix = g_mix_ref[...]
    w_in = w_in_ref[...]
    tail = jnp.zeros((TAIL_ROWS, LANES), F32)

    mproj = _dot(_rms(meta_ref[...], g_mix).astype(BF16), w_in)
    mk = mproj[:, OFF_K:OFF_K + KV_DIM]
    mv = mproj[:, OFF_V:OFF_V + KV_DIM]
    mk_ref[...] = mk
    mv_ref[...] = mv
    muc_ref[...] = mproj[:, OFF_C:OFF_C + CONV_DIM] * mproj[:, OFF_U:OFF_U + CONV_DIM]
    for j, layout in enumerate(_head_layouts(jnp.concatenate([mk, tail], axis=0))):
        mk2_ref[j] = layout
    for j, layout in enumerate(_head_layouts(jnp.concatenate([mv, tail], axis=0))):
        mv2_ref[j] = jnp.concatenate([layout, _denominator_lanes(META_ROWS, N_META + 1)], axis=1)

    _store_stacked_bias(bias_ref, (), _build_bias(ids_ref[...], table_ref, sinks_ref), seq)

    xs = xs_ref[...]
    proj = _dot(_rms(xs, g_mix).astype(BF16), w_in)
    uc = proj[:, OFF_C:OFF_C + CONV_DIM] * proj[:, OFF_U:OFF_U + CONV_DIM]
    sk_ref[...] = proj[:, OFF_K:OFF_K + KV_DIM]
    sv_ref[...] = proj[:, OFF_V:OFF_V + KV_DIM]
    q = proj[:, OFF_Q:OFF_Q + Q_DIM] * (HEAD_DIM ** -0.5)
    q_ref[...] = q.astype(BF16)
    qsw_ref[...] = _swap_lane_halves(q).astype(BF16)
    yc_ref[...] = proj[:, OFF_B:OFF_B + CONV_DIM]
    w0 = conv_w_ref[0:1, :]
    w1 = conv_w_ref[1:2, :]
    w2 = conv_w_ref[2:3, :]
    head = SUBLANES
    for b in range(n_batch):
        ucb_ref[b, head - 2:head, :] = state_ref[b]
        ucb_ref[b, head:head + seq, :] = uc[b * seq:(b + 1) * seq]
        sconv_ref[b] = uc[(b + 1) * seq - 2:(b + 1) * seq]
    den_lanes = _denominator_lanes(n_keys + TAIL_ROWS, n_keys + 1)

    def batch_body(b, carry):
        r0 = pl.multiple_of(b * seq, seq)
        rows = pl.ds(r0, seq)
        conv = (w0 * ucb_ref[b, head - 2:head - 2 + seq, :] + w1 * ucb_ref[b, head - 1:head - 1 + seq, :]
                + w2 * ucb_ref[b, head:head + seq, :])
        yc_ref[rows, :] = yc_ref[rows, :] * conv
        keys = _head_layouts(jnp.concatenate([cmk_ref[b], ck_ref[b], sk_ref[rows, :], tail], axis=0))
        vals = _head_layouts(jnp.concatenate([cmv_ref[b], cv_ref[b], sv_ref[rows, :], tail], axis=0))
        for j in range(N_KV_HEADS):
            c0 = j * GROUP * HEAD_DIM
            probs = _attend_scores(_stack_heads(q_ref, qsw_ref, rows, j), [keys[j]], bias_ref[j])
            pair0, pair1 = _attend_values(probs, [jnp.concatenate([vals[j], den_lanes], axis=1)], seq)
            ya_ref[rows, c0:c0 + LANES] = pair0
            ya_ref[rows, c0 + LANES:c0 + 2 * LANES] = pair1
        return carry

    lax.fori_loop(0, n_batch, batch_body, 0)

    yc_n = _rms(yc_ref[...], g_conv_ref[...]).astype(BF16)
    ya_n = _rms(ya_ref[...], g_attn_ref[...]).astype(BF16)
    h1_ref[...] = xs + _dot(yc_n, w_out_ref[0:CONV_DIM, :]) + _dot(ya_n, w_out_ref[CONV_DIM:, :])


def _smem():
    return pl.BlockSpec(memory_space=pltpu.SMEM)


def _aux_call(xs, meta, g_mix, w_in, conv_w, state, ck, cv, cmk, cmv, ids, table, sinks, g_conv, g_attn, w_out,
              *, n_batch, seq):
    n_tok = n_batch * seq
    n_keys = ids.shape[1] - TAIL_ROWS
    out_shape = (
        jax.ShapeDtypeStruct((n_tok, D_MODEL), F32),
        jax.ShapeDtypeStruct((n_tok, KV_DIM), F32),
        jax.ShapeDtypeStruct((n_tok, KV_DIM), F32),
        jax.ShapeDtypeStruct((n_batch, CONV_W - 1, CONV_DIM), F32),
        jax.ShapeDtypeStruct((N_KV_HEADS, META_ROWS, LANES), BF16),
        jax.ShapeDtypeStruct((N_KV_HEADS, META_ROWS, 2 * LANES), BF16),
        jax.ShapeDtypeStruct((N_META, KV_DIM), F32),
        jax.ShapeDtypeStruct((N_META, KV_DIM), F32),
        jax.ShapeDtypeStruct((N_META, CONV_DIM), F32),
    )
    vmem = pl.BlockSpec(memory_space=pltpu.VMEM)
    return pl.pallas_call(
        functools.partial(_aux_kernel, n_batch=n_batch, seq=seq, n_keys=n_keys),
        in_specs=[vmem] * 11 + [_smem(), _smem()] + [vmem] * 3,
        out_shape=out_shape,
        scratch_shapes=[
            pltpu.VMEM((n_batch, SUBLANES + seq, CONV_DIM), F32),
            pltpu.VMEM((n_tok, Q_DIM), BF16),
            pltpu.VMEM((n_tok, Q_DIM), BF16),
            pltpu.VMEM((n_tok, Q_DIM), F32),
            pltpu.VMEM((n_tok, CONV_DIM), F32),
            pltpu.VMEM((N_KV_HEADS, GROUP * seq, n_keys + TAIL_ROWS), F32),
        ],
        compiler_params=pltpu.CompilerParams(vmem_limit_bytes=VMEM_LIMIT_BYTES),
        name="aux_sample_mixer",
    )(xs, meta, g_mix, w_in, conv_w, state, ck, cv, cmk, cmv, ids, table, sinks, g_conv, g_attn, w_out)


def _layer_kernel(xnext_ref, xprev_ref, g_mix_ref, w_in_ref, conv_w_ref, mk2_ref, mv2_ref, muc_ref, ids_ref,
                  table_ref, sinks_ref, g_conv_ref, g_attn_ref, w_out_ref, g_mlp_ref, w_up_ref, w_down_ref,
                  g_final_ref,
                  y_ref, pk_ref, pv_ref, pc_ref,
                  ucb_ref, kb_ref, vb_ref, q_ref, qsw_ref, ya_ref, bias_ref, yn_ref, hn_ref, xn_ref, proj_ref,
                  *, tile, n_tiles, n_total):
    s = pl.program_id(0)

    @pl.when(s == 0)
    def _():
        for cls in range(WIN_CHUNKS + 1):
            _store_stacked_bias(bias_ref, (cls,), _build_bias(ids_ref[cls], table_ref, sinks_ref), CHUNK)
        ucb_ref[...] = jnp.zeros(ucb_ref.shape, F32)
        kb_ref[...] = jnp.zeros(kb_ref.shape, BF16)
        ones = _denominator_lanes(WINDOW + tile, WINDOW + tile)
        for sub in range(TILES_PER_STEP):
            for j in range(N_KV_HEADS):
                vb_ref[sub, j] = jnp.concatenate([jnp.zeros_like(ones), ones], axis=1)
        yn_ref[...] = jnp.zeros(yn_ref.shape, BF16)
        proj_ref[...] = jnp.zeros(proj_ref.shape, F32)

    for sub in range(TILES_PER_STEP):
        rows = pl.ds(sub * tile, tile)
        before = (sub - 1) % TILES_PER_STEP
        _layer_tile(s * TILES_PER_STEP - 1 + sub, sub == 0,
                    xnext_ref.at[rows], xprev_ref.at[rows], g_mix_ref, w_in_ref, conv_w_ref, mk2_ref, mv2_ref,
                    muc_ref, g_conv_ref, g_attn_ref, w_out_ref, g_mlp_ref, w_up_ref, w_down_ref, g_final_ref,
                    y_ref.at[rows], pk_ref, pv_ref, pc_ref,
                    ucb_ref.at[sub], kb_ref.at[sub], vb_ref.at[sub],
                    ucb_ref.at[before], kb_ref.at[before], vb_ref.at[before],
                    q_ref, qsw_ref, ya_ref, bias_ref, yn_ref, hn_ref, xn_ref, proj_ref,
                    tile=tile, n_tiles=n_tiles, n_total=n_total)


def _layer_tile(g, write_tails, xnext_ref, xprev_ref, g_mix_ref, w_in_ref, conv_w_ref, mk2_ref, mv2_ref,
                muc_ref, g_conv_ref, g_attn_ref, w_out_ref, g_mlp_ref, w_up_ref, w_down_ref, g_final_ref,
                y_ref, pk_ref, pv_ref, pc_ref,
                ucb_ref, kb_ref, vb_ref, ucb_before_ref, kb_before_ref, vb_before_ref,
                q_ref, qsw_ref, ya_ref, bias_ref, yn_ref, hn_ref, xn_ref, proj_ref,
                *, tile, n_tiles, n_total):
    t = lax.rem(jnp.clip(g, 0, n_total - 1), n_tiles)
    first = t == 0
    head = SUBLANES
    chunks = tile // CHUNK
    groups = chunks // CHUNK_GROUP
    pieces = _split_columns(D_FF, MLP_PIECES)
    hidden = {}

    def mlp_up(idx):
        n0, n1 = pieces[idx]
        u = jnp.maximum(_dot(hn_ref[...], w_up_ref[:, n0:n1]), 0.0)
        hidden[idx] = (u * u).astype(BF16)

    def mlp_down(acc, idx):
        n0, n1 = pieces[idx]
        return acc + _dot(hidden.pop(idx), w_down_ref[n0:n1, :])

    def scores(i):
        cls = jnp.where(first, i, WIN_CHUNKS) if i < WIN_CHUNKS else WIN_CHUNKS
        rows = slice(i * CHUNK, (i + 1) * CHUNK)
        band = slice(i * CHUNK, i * CHUNK + BAND)
        return [_attend_scores(_stack_heads(q_ref, qsw_ref, rows, j), [kb_ref[j, band, :], mk2_ref[j]],
                               bias_ref[cls, j]) for j in range(N_KV_HEADS)]

    def values(i, probs):
        rows = slice(i * CHUNK, (i + 1) * CHUNK)
        band = slice(i * CHUNK, i * CHUNK + BAND)
        for j in range(N_KV_HEADS):
            c0 = j * GROUP * HEAD_DIM
            pair0, pair1 = _attend_values(probs[j], [vb_ref[j, band, :], mv2_ref[j]], CHUNK)
            ya_ref[rows, c0:c0 + LANES] = pair0
            ya_ref[rows, c0 + LANES:c0 + 2 * LANES] = pair1

    h1 = xprev_ref[...] + _dot(yn_ref[...], w_out_ref[...])
    hn_ref[...] = _rms(h1, g_mlp_ref[...]).astype(BF16)

    xn_ref[...] = _rms(xnext_ref[...], g_mix_ref[...]).astype(BF16)

    uc = proj_ref[:, OFF_C:OFF_C + CONV_DIM] * proj_ref[:, OFF_U:OFF_U + CONV_DIM]
    k = proj_ref[:, OFF_K:OFF_K + KV_DIM]
    v = proj_ref[:, OFF_V:OFF_V + KV_DIM]

    ucb_ref[head - 2:head, :] = jnp.where(first, muc_ref[N_META - 2:N_META, :],
                                          ucb_before_ref[head + tile - 2:head + tile, :])
    zeros = jnp.zeros((N_KV_HEADS, WINDOW, LANES), BF16)
    kb_ref[:, 0:WINDOW, :] = jnp.where(first, zeros, kb_before_ref[:, tile:tile + WINDOW, :])
    vb_ref[:, 0:WINDOW, 0:LANES] = jnp.where(first, zeros, vb_before_ref[:, tile:tile + WINDOW, 0:LANES])

    ucb_ref[head:head + tile, :] = uc
    for j, layout in enumerate(_head_layouts(k)):
        kb_ref[j, WINDOW:WINDOW + tile, :] = layout
    for j, layout in enumerate(_head_layouts(v)):
        vb_ref[j, WINDOW:WINDOW + tile, 0:LANES] = layout
    q = proj_ref[:, OFF_Q:OFF_Q + Q_DIM] * (HEAD_DIM ** -0.5)
    q_ref[...] = q.astype(BF16)
    qsw_ref[...] = _swap_lane_halves(q).astype(BF16)

    if write_tails:
        pk_ref[...] = k[tile - WINDOW:tile]
        pv_ref[...] = v[tile - WINDOW:tile]
        pc_ref[...] = uc[tile - (CONV_W - 1):tile]

    conv = (conv_w_ref[0:1, :] * ucb_ref[head - 2:head - 2 + tile, :]
            + conv_w_ref[1:2, :] * ucb_ref[head - 1:head - 1 + tile, :]
            + conv_w_ref[2:3, :] * uc)
    yc = proj_ref[:, OFF_B:OFF_B + CONV_DIM] * conv
    yn_ref[:, 0:CONV_DIM] = _rms(yc, g_conv_ref[...]).astype(BF16)

    proj_ref[:, 0:OFF_Q] = _dot(xn_ref[...], w_in_ref[:, 0:OFF_Q])

    mlp_items = [("up", 0)]
    for idx in range(1, MLP_PIECES):
        mlp_items += [("up", idx), ("down", idx - 1)]
    mlp_items.append(("down", MLP_PIECES - 1))
    runs = _run_lengths(len(mlp_items), groups + 1)

    group_of = lambda g: range(g * CHUNK_GROUP, (g + 1) * CHUNK_GROUP)
    probs = [scores(i) for i in group_of(0)]
    h2 = h1
    items = iter(mlp_items)
    for g in range(groups + 1):
        for _ in range(runs[g]):
            kind, idx = next(items)
            if kind == "up":
                mlp_up(idx)
            else:
                h2 = mlp_down(h2, idx)
        if g < groups:
            for i, p in zip(group_of(g), probs):
                values(i, p)
        if g + 1 < groups:
            probs = [scores(i) for i in group_of(g + 1)]
    y_ref[...] = _rms(h2, g_final_ref[...])
    yn_ref[:, CONV_DIM:] = _rms(ya_ref[...], g_attn_ref[...]).astype(BF16)

    proj_ref[:, OFF_Q:] = _dot(xn_ref[...], w_in_ref[:, OFF_Q:])


def _resident(shape):
    return pl.BlockSpec(shape, lambda *_: (0,) * len(shape), pipeline_mode=pl.Buffered(1))


def _layer_call(x, g_mix, w_in, conv_w, mk2, mv2, muc, ids, table, sinks, g_conv, g_attn, w_out,
                g_mlp, w_up, w_down, g_final):
    n_batch, seq, _ = x.shape
    tile = LAYER_TILE
    block = tile * TILES_PER_STEP
    n_tiles = seq // tile
    n_total = n_batch * n_tiles
    n_blocks = n_total // TILES_PER_STEP
    blocks_per_row = seq // block
    assert seq % block == 0 and tile % (CHUNK * CHUNK_GROUP) == 0 and tile >= WINDOW
    n_keys = BAND + META_ROWS
    out_shape = (
        jax.ShapeDtypeStruct((n_batch, seq, D_MODEL), F32),
        jax.ShapeDtypeStruct((n_batch, WINDOW, KV_DIM), F32),
        jax.ShapeDtypeStruct((n_batch, WINDOW, KV_DIM), F32),
        jax.ShapeDtypeStruct((n_batch, CONV_W - 1, CONV_DIM), F32),
    )

    def x_block(offset):
        def index_map(s):
            b = jnp.clip(s + offset, 0, n_blocks - 1)
            return b // blocks_per_row, b % blocks_per_row, 0
        return index_map

    def tails_block(s):
        first_tile = jnp.clip(s * TILES_PER_STEP - 1, 0, n_total - 1)
        return first_tile // n_tiles, 0, 0

    per_batch = lambda shape: pl.BlockSpec((None,) + shape, tails_block)
    operands = (x, x, g_mix, w_in, conv_w, mk2, mv2, muc, ids, table, sinks, g_conv, g_attn, w_out,
                g_mlp, w_up, w_down, g_final)
    in_specs = [pl.BlockSpec((None, block, D_MODEL), x_block(0)), pl.BlockSpec((None, block, D_MODEL), x_block(-1))]
    in_specs += [_smem() if a is table or a is sinks else _resident(a.shape) for a in operands[2:]]
    return pl.pallas_call(
        functools.partial(_layer_kernel, tile=tile, n_tiles=n_tiles, n_total=n_total),
        grid=(n_blocks + 1,),
        in_specs=in_specs,
        out_specs=(
            pl.BlockSpec((None, block, D_MODEL), x_block(-1)),
            per_batch((WINDOW, KV_DIM)), per_batch((WINDOW, KV_DIM)), per_batch((CONV_W - 1, CONV_DIM)),
        ),
        out_shape=out_shape,
        scratch_shapes=[
            pltpu.VMEM((TILES_PER_STEP, SUBLANES + tile, CONV_DIM), F32),
            pltpu.VMEM((TILES_PER_STEP, N_KV_HEADS, WINDOW + tile, LANES), BF16),
            pltpu.VMEM((TILES_PER_STEP, N_KV_HEADS, WINDOW + tile, 2 * LANES), BF16),
            pltpu.VMEM((tile, Q_DIM), BF16),
            pltpu.VMEM((tile, Q_DIM), BF16),
            pltpu.VMEM((tile, Q_DIM), F32),
            pltpu.VMEM((WIN_CHUNKS + 1, N_KV_HEADS, GROUP * CHUNK, n_keys), F32),
            pltpu.VMEM((tile, CONV_DIM + Q_DIM), BF16),
            pltpu.VMEM((tile, D_MODEL), BF16),
            pltpu.VMEM((tile, D_MODEL), BF16),
            pltpu.VMEM((tile, IN_DIM), F32),
        ],
        compiler_params=pltpu.CompilerParams(
            dimension_semantics=("arbitrary",), vmem_limit_bytes=VMEM_LIMIT_BYTES),
        name="prompt_layer",
    )(*operands)


def _mlp_kernel(h_ref, g_mlp_ref, w_up_ref, w_down_ref, g_final_ref, y_ref):
    h = h_ref[...]
    u = jnp.maximum(_dot(_rms(h, g_mlp_ref[...]).astype(BF16), w_up_ref[...]), 0.0)
    h2 = h + _dot((u * u).astype(BF16), w_down_ref[...])
    y_ref[...] = _rms(h2, g_final_ref[...])


def _mlp_call(h, g_mlp, w_up, w_down, g_final):
    n_tok = h.shape[0]
    tile = min(MLP_TILE, n_tok)
    assert n_tok % tile == 0
    return pl.pallas_call(
        _mlp_kernel,
        grid=(n_tok // tile,),
        in_specs=[
            pl.BlockSpec((tile, D_MODEL), lambda i: (i, 0)),
            _resident(g_mlp.shape), _resident(w_up.shape), _resident(w_down.shape), _resident(g_final.shape),
        ],
        out_specs=pl.BlockSpec((tile, D_MODEL), lambda i: (i, 0)),
        out_shape=jax.ShapeDtypeStruct((n_tok, D_MODEL), F32),
        compiler_params=pltpu.CompilerParams(
            dimension_semantics=("arbitrary",), vmem_limit_bytes=VMEM_LIMIT_BYTES),
        name="mlp",
    )(h, g_mlp, w_up, w_down, g_final)


def _t5_bucket(rp):
    nb = N_BUCKETS // 2
    max_exact = nb // 2
    ret = jnp.where(rp > 0, nb, 0)
    n = jnp.abs(rp)
    nf = jnp.maximum(n, 1).astype(F32)
    large = max_exact + (jnp.log(nf / max_exact) / math.log(MAX_DISTANCE / max_exact)
                         * (nb - max_exact)).astype(jnp.int32)
    large = jnp.minimum(large, nb - 1)
    return ret + jnp.where(n < max_exact, n, large)


def _bucket_ids(q_pos, k_pos, valid=None):
    ids = _t5_bucket(k_pos[None, :] - q_pos[:, None])
    if valid is not None:
        ids = jnp.where(valid[None, :], ids, MASKED_ID)
    n_q = q_pos.shape[0]
    tail = jnp.concatenate([jnp.full((n_q, 1), SINK_ID, jnp.int32),
                            jnp.full((n_q, TAIL_ROWS - 1), MASKED_ID, jnp.int32)], axis=1)
    return jnp.concatenate([ids, tail], axis=1)


def _prompt_bucket_ids():
    qi = jnp.arange(CHUNK, dtype=jnp.int32)
    r = jnp.arange(BAND, dtype=jnp.int32)
    pm = jnp.arange(N_META, dtype=jnp.int32)
    out = []
    for c in range(WIN_CHUNKS + 1):
        frame_k = c * CHUNK - WIN_CHUNKS * CHUNK + r
        k_pos = jnp.concatenate([N_META + frame_k, pm])
        valid = jnp.concatenate([frame_k >= 0, jnp.ones((N_META,), bool)])
        out.append(_bucket_ids(N_META + c * CHUNK + qi, k_pos, valid))
    return jnp.stack(out)


def _sample_bucket_ids(n_win, seq):
    s = jnp.arange(seq, dtype=jnp.int32)
    k_pos = jnp.concatenate([jnp.arange(N_META, dtype=jnp.int32),
                             N_META + PAST_LEN - n_win + jnp.arange(n_win, dtype=jnp.int32),
                             N_META + PAST_LEN + s])
    return _bucket_ids(N_META + PAST_LEN + s, k_pos)


def kernel(x_prompt, x_sample, cache_k, cache_v, cache_meta_k, cache_meta_v, state_conv, meta_tokens,
           norm_mix, w_in, conv_w, attn_sinks, rel_bias_table, norm_conv_out, norm_attn_out, w_out,
           norm_mlp, w_up, w_down, norm_final):
    n_batch, seq, _ = x_prompt.shape
    s_batch, s_seq, _ = x_sample.shape
    n_win = cache_k.shape[2]
    row = lambda a: a.reshape(1, -1)

    w_in_b = w_in[0].astype(BF16)
    w_out_b = w_out[0].astype(BF16)
    w_up_b = w_up[0].astype(BF16)
    w_down_b = w_down[0].astype(BF16)
    g_mix, g_conv, g_attn = row(norm_mix[0]), row(norm_conv_out[0]), row(norm_attn_out[0])
    g_mlp, g_final = row(norm_mlp[0]), row(norm_final)
    table = rel_bias_table.astype(F32)
    sinks = attn_sinks[0].astype(F32)

    h1_s, s_k, s_v, s_conv, mk2, mv2, mk, mv, muc = _aux_call(
        x_sample.reshape(s_batch * s_seq, D_MODEL), meta_tokens, g_mix, w_in_b, conv_w[0], state_conv[0],
        cache_k[0].reshape(s_batch, n_win, KV_DIM), cache_v[0].reshape(s_batch, n_win, KV_DIM),
        cache_meta_k[0].reshape(s_batch, N_META, KV_DIM), cache_meta_v[0].reshape(s_batch, N_META, KV_DIM),
        _sample_bucket_ids(n_win, s_seq), table, sinks, g_conv, g_attn, w_out_b, n_batch=s_batch, seq=s_seq)

    y_prompt, p_k, p_v, p_conv = _layer_call(
        x_prompt, g_mix, w_in_b, conv_w[0], mk2, mv2, muc, _prompt_bucket_ids(), table, sinks,
        g_conv, g_attn, w_out_b, g_mlp, w_up_b, w_down_b, g_final)
    y_sample = _mlp_call(h1_s, g_mlp, w_up_b, w_down_b, g_final)

    kv_shape = lambda a, n, length: a.reshape(1, n, length, N_KV_HEADS, HEAD_DIM)
    meta_shape = (1, n_batch, N_META, N_KV_HEADS, HEAD_DIM)
    return (
        y_prompt,
        y_sample.reshape(s_batch, s_seq, D_MODEL),
        kv_shape(p_k, n_batch, WINDOW), kv_shape(p_v, n_batch, WINDOW),
        jnp.broadcast_to(mk.reshape(1, 1, N_META, N_KV_HEADS, HEAD_DIM), meta_shape),
        jnp.broadcast_to(mv.reshape(1, 1, N_META, N_KV_HEADS, HEAD_DIM), meta_shape),
        p_conv[None],
        kv_shape(s_k, s_batch, s_seq), kv_shape(s_v, s_batch, s_seq),
        s_conv[None],
    )
```

```python
import functools
import math

import jax
import jax.numpy as jnp
from jax import lax
from jax.experimental import pallas as pl
from jax.experimental.pallas import tpu as pltpu

D_MODEL = 1024
CHUNK = 64
N_META = 16
CONV_DIM = 512
CONV_W = 3
N_HEADS = 8
N_KV_HEADS = 2
HEAD_DIM = 64
GROUP = N_HEADS // N_KV_HEADS
Q_DIM = N_HEADS * HEAD_DIM
KV_DIM = N_KV_HEADS * HEAD_DIM
IN_DIM = 3 * CONV_DIM + Q_DIM + 2 * KV_DIM
WINDOW = 128
WIN_CHUNKS = WINDOW // CHUNK
BAND = (WIN_CHUNKS + 1) * CHUNK
N_BUCKETS = 32
MAX_DISTANCE = 128
D_FF = 4 * D_MODEL
EPS = 1e-6
PAST_LEN = 4096

OFF_B, OFF_C, OFF_U = 0, CONV_DIM, 2 * CONV_DIM
OFF_Q = 3 * CONV_DIM
OFF_K = OFF_Q + Q_DIM
OFF_V = OFF_K + KV_DIM

LANES = 128
SUBLANES = 8
BF16_ROWS = 16
MXU_TILE = 256
VMEM_LIMIT_BYTES = 56 * 1024 * 1024
MASK_VALUE = -1e30

LAYER_TILE = 256
TILES_PER_STEP = 2
CHUNK_GROUP = 2
MLP_PIECES = 8
TAIL_ROWS = BF16_ROWS
META_ROWS = N_META + TAIL_ROWS
MASKED_ID = -1
SINK_ID = -2
HEAD_ORDER = (0, 2, 1, 3)

BF16 = jnp.bfloat16
F32 = jnp.float32


def _rms(x, g):
    ms = jnp.mean(x * x, axis=-1, keepdims=True)
    return x * lax.rsqrt(ms + EPS) * g


def _dot(a, b):
    return jnp.dot(a, b, preferred_element_type=F32)


def _dot_nt(a, b):
    return lax.dot_general(a, b, (((1,), (1,)), ((), ())), preferred_element_type=F32)


def _swap_lane_halves(x):
    pairs = [pltpu.roll(x[:, c:c + LANES], HEAD_DIM, axis=1) for c in range(0, x.shape[1], LANES)]
    return jnp.concatenate(pairs, axis=1)


def _head_layouts(kv):
    lane = lax.broadcasted_iota(jnp.int32, kv.shape, 1)
    low = lane < HEAD_DIM
    zero = jnp.zeros_like(kv)
    return (jnp.where(low, kv, zero).astype(BF16),
            jnp.where(low, pltpu.roll(kv, HEAD_DIM, axis=1), zero).astype(BF16))


def _denominator_lanes(n_rows, n_counted):
    row = lax.broadcasted_iota(jnp.int32, (n_rows, LANES), 0)
    return jnp.where(row < n_counted, 1.0, 0.0).astype(BF16)


def _build_bias(ids, table_ref, sinks_ref):
    out = []
    for h in range(N_HEADS):
        acc = jnp.full(ids.shape, MASK_VALUE, F32)
        for bucket in range(N_BUCKETS):
            acc = jnp.where(ids == bucket, table_ref[bucket, h], acc)
        out.append(jnp.where(ids == SINK_ID, sinks_ref[h], acc))
    return out


def _store_stacked_bias(bias_ref, prefix, per_head, q_rows):
    for h, b in enumerate(per_head):
        j, g = divmod(h, GROUP)
        r0 = HEAD_ORDER.index(g) * q_rows
        bias_ref[prefix + (j, slice(r0, r0 + q_rows))] = b


def _run_lengths(n_items, n_runs):
    base, extra = divmod(n_items, n_runs)
    return [base + (1 if i >= n_runs - extra else 0) for i in range(n_runs)]


def _split_columns(width, n_parts):
    size, rem = divmod(width, n_parts)
    assert rem == 0 and size % MXU_TILE == 0
    return [(i * size, (i + 1) * size) for i in range(n_parts)]


def _stack_heads(q_ref, qsw_ref, rows, j):
    c0 = j * GROUP * HEAD_DIM
    return jnp.concatenate([q_ref[rows, c0:c0 + LANES], q_ref[rows, c0 + LANES:c0 + 2 * LANES],
                            qsw_ref[rows, c0:c0 + LANES], qsw_ref[rows, c0 + LANES:c0 + 2 * LANES]], axis=0)


def _attend_scores(q_heads, key_parts, bias):
    s = _dot_nt(q_heads, jnp.concatenate(key_parts, axis=0)) + bias
    return jnp.exp(s - jnp.max(s, axis=-1, keepdims=True)).astype(BF16)


def _attend_values(probs, value_parts, q_rows):
    o = _dot(probs, jnp.concatenate(value_parts, axis=0))
    o = o[:, :LANES] / o[:, LANES:]
    even, odd = o[:2 * q_rows], pltpu.roll(o[2 * q_rows:], HEAD_DIM, axis=1)
    pairs = even + odd
    return pairs[:q_rows], pairs[q_rows:]


def _aux_kernel(xs_ref, meta_ref, g_mix_ref, w_in_ref, conv_w_ref, state_ref, ck_ref, cv_ref,
                cmk_ref, cmv_ref, ids_ref, table_ref, sinks_ref, g_conv_ref, g_attn_ref, w_out_ref,
                g_mlp_ref, w_up_hbm_ref, w_down_hbm_ref, g_final_ref,
                ys_ref, sk_ref, sv_ref, sconv_ref, mk2_ref, mv2_ref, mk_ref, mv_ref, muc_ref,
                ucb_ref, q_ref, qsw_ref, ya_ref, yc_ref, bias_ref, w_up_ref, w_down_ref, dma_sem,
                *, n_batch, seq, n_keys):
    up_copy = pltpu.make_async_copy(w_up_hbm_ref, w_up_ref, dma_sem.at[0])
    down_copy = pltpu.make_async_copy(w_down_hbm_ref, w_down_ref, dma_sem.at[1])
    up_copy.start()
    down_copy.start()

    g_mix = g_mix_ref[...]
    w_in = w_in_ref[...]
    tail = jnp.zeros((TAIL_ROWS, LANES), F32)

    mproj = _dot(_rms(meta_ref[...], g_mix).astype(BF16), w_in)
    mk = mproj[:, OFF_K:OFF_K + KV_DIM]
    mv = mproj[:, OFF_V:OFF_V + KV_DIM]
    mk_ref[...] = mk
    mv_ref[...] = mv
    muc_ref[...] = mproj[:, OFF_C:OFF_C + CONV_DIM] * mproj[:, OFF_U:OFF_U + CONV_DIM]
    for j, layout in enumerate(_head_layouts(jnp.concatenate([mk, tail], axis=0))):
        mk2_ref[j] = layout
    for j, layout in enumerate(_head_layouts(jnp.concatenate([mv, tail], axis=0))):
        mv2_ref[j] = jnp.concatenate([layout, _denominator_lanes(META_ROWS, N_META + 1)], axis=1)

    _store_stacked_bias(bias_ref, (), _build_bias(ids_ref[...], table_ref, sinks_ref), seq)

    xs = xs_ref[...]
    proj = _dot(_rms(xs, g_mix).astype(BF16), w_in)
    uc = proj[:, OFF_C:OFF_C + CONV_DIM] * proj[:, OFF_U:OFF_U + CONV_DIM]
    sk_ref[...] = proj[:, OFF_K:OFF_K + KV_DIM]
    sv_ref[...] = proj[:, OFF_V:OFF_V + KV_DIM]
    q = proj[:, OFF_Q:OFF_Q + Q_DIM] * (HEAD_DIM ** -0.5)
    q_ref[...] = q.astype(BF16)
    qsw_ref[...] = _swap_lane_halves(q).astype(BF16)
    yc_ref[...] = proj[:, OFF_B:OFF_B + CONV_DIM]
    w0 = conv_w_ref[0:1, :]
    w1 = conv_w_ref[1:2, :]
    w2 = conv_w_ref[2:3, :]
    head = SUBLANES
    for b in range(n_batch):
        ucb_ref[b, head - 2:head, :] = state_ref[b]
        ucb_ref[b, head:head + seq, :] = uc[b * seq:(b + 1) * seq]
        sconv_ref[b] = uc[(b + 1) * seq - 2:(b + 1) * seq]
    den_lanes = _denominator_lanes(n_keys + TAIL_ROWS, n_keys + 1)

    def batch_body(b, carry):
        r0 = pl.multiple_of(b * seq, seq)
        rows = pl.ds(r0, seq)
        conv = (w0 * ucb_ref[b, head - 2:head - 2 + seq, :] + w1 * ucb_ref[b, head - 1:head - 1 + seq, :]
                + w2 * ucb_ref[b, head:head + seq, :])
        yc_ref[rows, :] = yc_ref[rows, :] * conv
        keys = _head_layouts(jnp.concatenate([cmk_ref[b], ck_ref[b], sk_ref[rows, :], tail], axis=0))
        vals = _head_layouts(jnp.concatenate([cmv_ref[b], cv_ref[b], sv_ref[rows, :], tail], axis=0))
        for j in range(N_KV_HEADS):
            c0 = j * GROUP * HEAD_DIM
            probs = _attend_scores(_stack_heads(q_ref, qsw_ref, rows, j), [keys[j]], bias_ref[j])
            pair0, pair1 = _attend_values(probs, [jnp.concatenate([vals[j], den_lanes], axis=1)], seq)
            ya_ref[rows, c0:c0 + LANES] = pair0
            ya_ref[rows, c0 + LANES:c0 + 2 * LANES] = pair1
        return carry

    lax.fori_loop(0, n_batch, batch_body, 0)

    yc_n = _rms(yc_ref[...], g_conv_ref[...]).astype(BF16)
    ya_n = _rms(ya_ref[...], g_attn_ref[...]).astype(BF16)
    h1 = xs + _dot(yc_n, w_out_ref[0:CONV_DIM, :]) + _dot(ya_n, w_out_ref[CONV_DIM:, :])

    up_copy.wait()
    u = jnp.maximum(_dot(_rms(h1, g_mlp_ref[...]).astype(BF16), w_up_ref[...]), 0.0)
    down_copy.wait()
    h2 = h1 + _dot((u * u).astype(BF16), w_down_ref[...])
    ys_ref[...] = _rms(h2, g_final_ref[...])


def _smem():
    return pl.BlockSpec(memory_space=pltpu.SMEM)


def _aux_call(xs, meta, g_mix, w_in, conv_w, state, ck, cv, cmk, cmv, ids, table, sinks, g_conv, g_attn, w_out,
              g_mlp, w_up, w_down, g_final, *, n_batch, seq):
    n_tok = n_batch * seq
    n_keys = ids.shape[1] - TAIL_ROWS
    out_shape = (
        jax.ShapeDtypeStruct((n_tok, D_MODEL), F32),
        jax.ShapeDtypeStruct((n_tok, KV_DIM), F32),
        jax.ShapeDtypeStruct((n_tok, KV_DIM), F32),
        jax.ShapeDtypeStruct((n_batch, CONV_W - 1, CONV_DIM), F32),
        jax.ShapeDtypeStruct((N_KV_HEADS, META_ROWS, LANES), BF16),
        jax.ShapeDtypeStruct((N_KV_HEADS, META_ROWS, 2 * LANES), BF16),
        jax.ShapeDtypeStruct((N_META, KV_DIM), F32),
        jax.ShapeDtypeStruct((N_META, KV_DIM), F32),
        jax.ShapeDtypeStruct((N_META, CONV_DIM), F32),
    )
    vmem = pl.BlockSpec(memory_space=pltpu.VMEM)
    hbm = pl.BlockSpec(memory_space=pl.ANY)
    return pl.pallas_call(
        functools.partial(_aux_kernel, n_batch=n_batch, seq=seq, n_keys=n_keys),
        in_specs=[vmem] * 11 + [_smem(), _smem()] + [vmem] * 4 + [hbm, hbm, vmem],
        out_shape=out_shape,
        scratch_shapes=[
            pltpu.VMEM((n_batch, SUBLANES + seq, CONV_DIM), F32),
            pltpu.VMEM((n_tok, Q_DIM), BF16),
            pltpu.VMEM((n_tok, Q_DIM), BF16),
            pltpu.VMEM((n_tok, Q_DIM), F32),
            pltpu.VMEM((n_tok, CONV_DIM), F32),
            pltpu.VMEM((N_KV_HEADS, GROUP * seq, n_keys + TAIL_ROWS), F32),
            pltpu.VMEM(w_up.shape, w_up.dtype),
            pltpu.VMEM(w_down.shape, w_down.dtype),
            pltpu.SemaphoreType.DMA((2,)),
        ],
        compiler_params=pltpu.CompilerParams(vmem_limit_bytes=VMEM_LIMIT_BYTES),
        name="aux_sample_layer",
    )(xs, meta, g_mix, w_in, conv_w, state, ck, cv, cmk, cmv, ids, table, sinks, g_conv, g_attn, w_out,
      g_mlp, w_up, w_down, g_final)


def _layer_kernel(xnext_ref, xprev_ref, g_mix_ref, w_in_ref, conv_w_ref, mk2_ref, mv2_ref, muc_ref, ids_ref,
                  table_ref, sinks_ref, g_conv_ref, g_attn_ref, w_out_ref, g_mlp_ref, w_up_ref, w_down_ref,
                  g_final_ref,
                  y_ref, pk_ref, pv_ref, pc_ref,
                  ucb_ref, kb_ref, vb_ref, q_ref, qsw_ref, ya_ref, bias_ref, yn_ref, hn_ref, xn_ref, proj_ref,
                  *, tile, n_tiles, n_total):
    s = pl.program_id(0)

    @pl.when(s == 0)
    def _():
        for cls in range(WIN_CHUNKS + 1):
            _store_stacked_bias(bias_ref, (cls,), _build_bias(ids_ref[cls], table_ref, sinks_ref), CHUNK)
        ucb_ref[...] = jnp.zeros(ucb_ref.shape, F32)
        kb_ref[...] = jnp.zeros(kb_ref.shape, BF16)
        ones = _denominator_lanes(WINDOW + tile, WINDOW + tile)
        for sub in range(TILES_PER_STEP):
            for j in range(N_KV_HEADS):
                vb_ref[sub, j] = jnp.concatenate([jnp.zeros_like(ones), ones], axis=1)
        yn_ref[...] = jnp.zeros(yn_ref.shape, BF16)
        proj_ref[...] = jnp.zeros(proj_ref.shape, F32)

    for sub in range(TILES_PER_STEP):
        rows = pl.ds(sub * tile, tile)
        before = (sub - 1) % TILES_PER_STEP
        _layer_tile(s * TILES_PER_STEP - 1 + sub, sub == 0,
                    xnext_ref.at[rows], xprev_ref.at[rows], g_mix_ref, w_in_ref, conv_w_ref, mk2_ref, mv2_ref,
                    muc_ref, g_conv_ref, g_attn_ref, w_out_ref, g_mlp_ref, w_up_ref, w_down_ref, g_final_ref,
                    y_ref.at[rows], pk_ref, pv_ref, pc_ref,
                    ucb_ref.at[sub], kb_ref.at[sub], vb_ref.at[sub],
                    ucb_ref.at[before], kb_ref.at[before], vb_ref.at[before],
                    q_ref, qsw_ref, ya_ref, bias_ref, yn_ref, hn_ref, xn_ref, proj_ref,
                    tile=tile, n_tiles=n_tiles, n_total=n_total)


def _layer_tile(g, write_tails, xnext_ref, xprev_ref, g_mix_ref, w_in_ref, conv_w_ref, mk2_ref, mv2_ref,
                muc_ref, g_conv_ref, g_attn_ref, w_out_ref, g_mlp_ref, w_up_ref, w_down_ref, g_final_ref,
                y_ref, pk_ref, pv_ref, pc_ref,
                ucb_ref, kb_ref, vb_ref, ucb_before_ref, kb_before_ref, vb_before_ref,
                q_ref, qsw_ref, ya_ref, bias_ref, yn_ref, hn_ref, xn_ref, proj_ref,
                *, tile, n_tiles, n_total):
    t = lax.rem(jnp.clip(g, 0, n_total - 1), n_tiles)
    first = t == 0
    head = SUBLANES
    chunks = tile // CHUNK
    groups = chunks // CHUNK_GROUP
    pieces = _split_columns(D_FF, MLP_PIECES)
    hidden = {}

    def mlp_up(idx):
        n0, n1 = pieces[idx]
        u = jnp.maximum(_dot(hn_ref[...], w_up_ref[:, n0:n1]), 0.0)
        hidden[idx] = (u * u).astype(BF16)

    def mlp_down(acc, idx):
        n0, n1 = pieces[idx]
        return acc + _dot(hidden.pop(idx), w_down_ref[n0:n1, :])

    def scores(i):
        cls = jnp.where(first, i, WIN_CHUNKS) if i < WIN_CHUNKS else WIN_CHUNKS
        rows = slice(i * CHUNK, (i + 1) * CHUNK)
        band = slice(i * CHUNK, i * CHUNK + BAND)
        return [_attend_scores(_stack_heads(q_ref, qsw_ref, rows, j), [kb_ref[j, band, :], mk2_ref[j]],
                               bias_ref[cls, j]) for j in range(N_KV_HEADS)]

    def values(i, probs):
        rows = slice(i * CHUNK, (i + 1) * CHUNK)
        band = slice(i * CHUNK, i * CHUNK + BAND)
        for j in range(N_KV_HEADS):
            c0 = j * GROUP * HEAD_DIM
            pair0, pair1 = _attend_values(probs[j], [vb_ref[j, band, :], mv2_ref[j]], CHUNK)
            ya_ref[rows, c0:c0 + LANES] = pair0
            ya_ref[rows, c0 + LANES:c0 + 2 * LANES] = pair1

    h1 = xprev_ref[...] + _dot(yn_ref[...], w_out_ref[...])
    hn_ref[...] = _rms(h1, g_mlp_ref[...]).astype(BF16)

    xn_ref[...] = _rms(xnext_ref[...], g_mix_ref[...]).astype(BF16)

    uc = proj_ref[:, OFF_C:OFF_C + CONV_DIM] * proj_ref[:, OFF_U:OFF_U + CONV_DIM]
    k = proj_ref[:, OFF_K:OFF_K + KV_DIM]
    v = proj_ref[:, OFF_V:OFF_V + KV_DIM]

    ucb_ref[head - 2:head, :] = jnp.where(first, muc_ref[N_META - 2:N_META, :],
                                          ucb_before_ref[head + tile - 2:head + tile, :])
    zeros = jnp.zeros((N_KV_HEADS, WINDOW, LANES), BF16)
    kb_ref[:, 0:WINDOW, :] = jnp.where(first, zeros, kb_before_ref[:, tile:tile + WINDOW, :])
    vb_ref[:, 0:WINDOW, 0:LANES] = jnp.where(first, zeros, vb_before_ref[:, tile:tile + WINDOW, 0:LANES])

    ucb_ref[head:head + tile, :] = uc
    for j, layout in enumerate(_head_layouts(k)):
        kb_ref[j, WINDOW:WINDOW + tile, :] = layout
    for j, layout in enumerate(_head_layouts(v)):
        vb_ref[j, WINDOW:WINDOW + tile, 0:LANES] = layout
    q = proj_ref[:, OFF_Q:OFF_Q + Q_DIM] * (HEAD_DIM ** -0.5)
    q_ref[...] = q.astype(BF16)
    qsw_ref[...] = _swap_lane_halves(q).astype(BF16)

    if write_tails:
        pk_ref[...] = k[tile - WINDOW:tile]
        pv_ref[...] = v[tile - WINDOW:tile]
        pc_ref[...] = uc[tile - (CONV_W - 1):tile]

    conv = (conv_w_ref[0:1, :] * ucb_ref[head - 2:head - 2 + tile, :]
            + conv_w_ref[1:2, :] * ucb_ref[head - 1:head - 1 + tile, :]
            + conv_w_ref[2:3, :] * uc)
    yc = proj_ref[:, OFF_B:OFF_B + CONV_DIM] * conv
    yn_ref[:, 0:CONV_DIM] = _rms(yc, g_conv_ref[...]).astype(BF16)

    proj_ref[:, 0:OFF_Q] = _dot(xn_ref[...], w_in_ref[:, 0:OFF_Q])

    mlp_items = [("up", 0)]
    for idx in range(1, MLP_PIECES):
        mlp_items += [("up", idx), ("down", idx - 1)]
    mlp_items.append(("down", MLP_PIECES - 1))
    runs = _run_lengths(len(mlp_items), groups + 1)

    group_of = lambda g: range(g * CHUNK_GROUP, (g + 1) * CHUNK_GROUP)
    probs = [scores(i) for i in group_of(0)]
    h2 = h1
    items = iter(mlp_items)
    for g in range(groups + 1):
        for _ in range(runs[g]):
            kind, idx = next(items)
            if kind == "up":
                mlp_up(idx)
            else:
                h2 = mlp_down(h2, idx)
        if g < groups:
            for i, p in zip(group_of(g), probs):
                values(i, p)
        if g + 1 < groups:
            probs = [scores(i) for i in group_of(g + 1)]
    y_ref[...] = _rms(h2, g_final_ref[...])
    yn_ref[:, CONV_DIM:] = _rms(ya_ref[...], g_attn_ref[...]).astype(BF16)

    proj_ref[:, OFF_Q:] = _dot(xn_ref[...], w_in_ref[:, OFF_Q:])


def _resident(shape):
    return pl.BlockSpec(shape, lambda *_: (0,) * len(shape), pipeline_mode=pl.Buffered(1))


def _layer_call(x, g_mix, w_in, conv_w, mk2, mv2, muc, ids, table, sinks, g_conv, g_attn, w_out,
                g_mlp, w_up, w_down, g_final):
    n_batch, seq, _ = x.shape
    tile = LAYER_TILE
    block = tile * TILES_PER_STEP
    n_tiles = seq // tile
    n_total = n_batch * n_tiles
    n_blocks = n_total // TILES_PER_STEP
    blocks_per_row = seq // block
    assert seq % block == 0 and tile % (CHUNK * CHUNK_GROUP) == 0 and tile >= WINDOW
    n_keys = BAND + META_ROWS
    out_shape = (
        jax.ShapeDtypeStruct((n_batch, seq, D_MODEL), F32),
        jax.ShapeDtypeStruct((n_batch, WINDOW, KV_DIM), F32),
        jax.ShapeDtypeStruct((n_batch, WINDOW, KV_DIM), F32),
        jax.ShapeDtypeStruct((n_batch, CONV_W - 1, CONV_DIM), F32),
    )

    def x_block(offset):
        def index_map(s):
            b = jnp.clip(s + offset, 0, n_blocks - 1)
            return b // blocks_per_row, b % blocks_per_row, 0
        return index_map

    def tails_block(s):
        first_tile = jnp.clip(s * TILES_PER_STEP - 1, 0, n_total - 1)
        return first_tile // n_tiles, 0, 0

    per_batch = lambda shape: pl.BlockSpec((None,) + shape, tails_block)
    operands = (x, x, g_mix, w_in, conv_w, mk2, mv2, muc, ids, table, sinks, g_conv, g_attn, w_out,
                g_mlp, w_up, w_down, g_final)
    in_specs = [pl.BlockSpec((None, block, D_MODEL), x_block(0)), pl.BlockSpec((None, block, D_MODEL), x_block(-1))]
    in_specs += [_smem() if a is table or a is sinks else _resident(a.shape) for a in operands[2:]]
    return pl.pallas_call(
        functools.partial(_layer_kernel, tile=tile, n_tiles=n_tiles, n_total=n_total),
        grid=(n_blocks + 1,),
        in_specs=in_specs,
        out_specs=(
            pl.BlockSpec((None, block, D_MODEL), x_block(-1)),
            per_batch((WINDOW, KV_DIM)), per_batch((WINDOW, KV_DIM)), per_batch((CONV_W - 1, CONV_DIM)),
        ),
        out_shape=out_shape,
        scratch_shapes=[
            pltpu.VMEM((TILES_PER_STEP, SUBLANES + tile, CONV_DIM), F32),
            pltpu.VMEM((TILES_PER_STEP, N_KV_HEADS, WINDOW + tile, LANES), BF16),
            pltpu.VMEM((TILES_PER_STEP, N_KV_HEADS, WINDOW + tile, 2 * LANES), BF16),
            pltpu.VMEM((tile, Q_DIM), BF16),
            pltpu.VMEM((tile, Q_DIM), BF16),
            pltpu.VMEM((tile, Q_DIM), F32),
            pltpu.VMEM((WIN_CHUNKS + 1, N_KV_HEADS, GROUP * CHUNK, n_keys), F32),
            pltpu.VMEM((tile, CONV_DIM + Q_DIM), BF16),
            pltpu.VMEM((tile, D_MODEL), BF16),
            pltpu.VMEM((tile, D_MODEL), BF16),
            pltpu.VMEM((tile, IN_DIM), F32),
        ],
        compiler_params=pltpu.CompilerParams(
            dimension_semantics=("arbitrary",), vmem_limit_bytes=VMEM_LIMIT_BYTES),
        name="prompt_layer",
    )(*operands)


def _t5_bucket(rp):
    nb = N_BUCKETS // 2
    max_exact = nb // 2
    ret = jnp.where(rp > 0, nb, 0)
    n = jnp.abs(rp)
    nf = jnp.maximum(n, 1).astype(F32)
    large = max_exact + (jnp.log(nf / max_exact) / math.log(MAX_DISTANCE / max_exact)
                         * (nb - max_exact)).astype(jnp.int32)
    large = jnp.minimum(large, nb - 1)
    return ret + jnp.where(n < max_exact, n, large)


def _bucket_ids(q_pos, k_pos, valid=None):
    ids = _t5_bucket(k_pos[None, :] - q_pos[:, None])
    if valid is not None:
        ids = jnp.where(valid[None, :], ids, MASKED_ID)
    n_q = q_pos.shape[0]
    tail = jnp.concatenate([jnp.full((n_q, 1), SINK_ID, jnp.int32),
                            jnp.full((n_q, TAIL_ROWS - 1), MASKED_ID, jnp.int32)], axis=1)
    return jnp.concatenate([ids, tail], axis=1)


def _prompt_bucket_ids():
    qi = jnp.arange(CHUNK, dtype=jnp.int32)
    r = jnp.arange(BAND, dtype=jnp.int32)
    pm = jnp.arange(N_META, dtype=jnp.int32)
    out = []
    for c in range(WIN_CHUNKS + 1):
        frame_k = c * CHUNK - WIN_CHUNKS * CHUNK + r
        k_pos = jnp.concatenate([N_META + frame_k, pm])
        valid = jnp.concatenate([frame_k >= 0, jnp.ones((N_META,), bool)])
        out.append(_bucket_ids(N_META + c * CHUNK + qi, k_pos, valid))
    return jnp.stack(out)


def _sample_bucket_ids(n_win, seq):
    s = jnp.arange(seq, dtype=jnp.int32)
    k_pos = jnp.concatenate([jnp.arange(N_META, dtype=jnp.int32),
                             N_META + PAST_LEN - n_win + jnp.arange(n_win, dtype=jnp.int32),
                             N_META + PAST_LEN + s])
    return _bucket_ids(N_META + PAST_LEN + s, k_pos)


def kernel(x_prompt, x_sample, cache_k, cache_v, cache_meta_k, cache_meta_v, state_conv, meta_tokens,
           norm_mix, w_in, conv_w, attn_sinks, rel_bias_table, norm_conv_out, norm_attn_out, w_out,
           norm_mlp, w_up, w_down, norm_final):
    n_batch, seq, _ = x_prompt.shape
    s_batch, s_seq, _ = x_sample.shape
    n_win = cache_k.shape[2]
    row = lambda a: a.reshape(1, -1)

    w_in_b = w_in[0].astype(BF16)
    w_out_b = w_out[0].astype(BF16)
    w_up_b = w_up[0].astype(BF16)
    w_down_b = w_down[0].astype(BF16)
    g_mix, g_conv, g_attn = row(norm_mix[0]), row(norm_conv_out[0]), row(norm_attn_out[0])
    g_mlp, g_final = row(norm_mlp[0]), row(norm_final)
    table = rel_bias_table.astype(F32)
    sinks = attn_sinks[0].astype(F32)

    y_sample, s_k, s_v, s_conv, mk2, mv2, mk, mv, muc = _aux_call(
        x_sample.reshape(s_batch * s_seq, D_MODEL), meta_tokens, g_mix, w_in_b, conv_w[0], state_conv[0],
        cache_k[0].reshape(s_batch, n_win, KV_DIM), cache_v[0].reshape(s_batch, n_win, KV_DIM),
        cache_meta_k[0].reshape(s_batch, N_META, KV_DIM), cache_meta_v[0].reshape(s_batch, N_META, KV_DIM),
        _sample_bucket_ids(n_win, s_seq), table, sinks, g_conv, g_attn, w_out_b,
        g_mlp, w_up_b, w_down_b, g_final, n_batch=s_batch, seq=s_seq)

    y_prompt, p_k, p_v, p_conv = _layer_call(
        x_prompt, g_mix, w_in_b, conv_w[0], mk2, mv2, muc, _prompt_bucket_ids(), table, sinks,
        g_conv, g_attn, w_out_b, g_mlp, w_up_b, w_down_b, g_final)

    kv_shape = lambda a, n, length: a.reshape(1, n, length, N_KV_HEADS, HEAD_DIM)
    meta_shape = (1, n_batch, N_META, N_KV_HEADS, HEAD_DIM)
    return (
        y_prompt,
        y_sample.reshape(s_batch, s_seq, D_MODEL),
        kv_shape(p_k, n_batch, WINDOW), kv_shape(p_v, n_batch, WINDOW),
        jnp.broadcast_to(mk.reshape(1, 1, N_META, N_KV_HEADS, HEAD_DIM), meta_shape),
        jnp.broadcast_to(mv.reshape(1, 1, N_META, N_KV_HEADS, HEAD_DIM), meta_shape),
        p_conv[None],
        kv_shape(s_k, s_batch, s_seq), kv_shape(s_v, s_batch, s_seq),
        s_conv[None],
    )
```

```python
import functools
import math

import jax
import jax.numpy as jnp
from jax import lax
from jax.experimental import pallas as pl
from jax.experimental.pallas import tpu as pltpu

D_MODEL = 1024
CHUNK = 64
N_META = 16
CONV_DIM = 512
CONV_W = 3
N_HEADS = 8
N_KV_HEADS = 2
HEAD_DIM = 64
GROUP = N_HEADS // N_KV_HEADS
Q_DIM = N_HEADS * HEAD_DIM
KV_DIM = N_KV_HEADS * HEAD_DIM
IN_DIM = 3 * CONV_DIM + Q_DIM + 2 * KV_DIM
WINDOW = 128
WIN_CHUNKS = WINDOW // CHUNK
BAND = (WIN_CHUNKS + 1) * CHUNK
N_BUCKETS = 32
MAX_DISTANCE = 128
D_FF = 4 * D_MODEL
EPS = 1e-6
PAST_LEN = 4096

OFF_B, OFF_C, OFF_U = 0, CONV_DIM, 2 * CONV_DIM
OFF_Q = 3 * CONV_DIM
OFF_K = OFF_Q + Q_DIM
OFF_V = OFF_K + KV_DIM

LANES = 128
SUBLANES = 8
BF16_ROWS = 16
MXU_TILE = 256
VMEM_LIMIT_BYTES = 56 * 1024 * 1024
MASK_VALUE = -1e30

LAYER_TILE = 256
TILES_PER_STEP = 2
CHUNK_GROUP = 2
MLP_PIECES = 4
TAIL_ROWS = BF16_ROWS
META_ROWS = N_META + TAIL_ROWS
MASKED_ID = -1
SINK_ID = -2
HEAD_ORDER = (0, 2, 1, 3)

BF16 = jnp.bfloat16
F32 = jnp.float32


def _rms(x, g):
    ms = jnp.mean(x * x, axis=-1, keepdims=True)
    return x * lax.rsqrt(ms + EPS) * g


def _dot(a, b):
    return jnp.dot(a, b, preferred_element_type=F32)


def _dot_nt(a, b):
    return lax.dot_general(a, b, (((1,), (1,)), ((), ())), preferred_element_type=F32)


def _swap_lane_halves(x):
    pairs = [pltpu.roll(x[:, c:c + LANES], HEAD_DIM, axis=1) for c in range(0, x.shape[1], LANES)]
    return jnp.concatenate(pairs, axis=1)


def _head_layouts(kv):
    lane = lax.broadcasted_iota(jnp.int32, kv.shape, 1)
    low = lane < HEAD_DIM
    zero = jnp.zeros_like(kv)
    return (jnp.where(low, kv, zero).astype(BF16),
            jnp.where(low, pltpu.roll(kv, HEAD_DIM, axis=1), zero).astype(BF16))


def _denominator_lanes(n_rows, n_counted):
    row = lax.broadcasted_iota(jnp.int32, (n_rows, LANES), 0)
    return jnp.where(row < n_counted, 1.0, 0.0).astype(BF16)


def _build_bias(ids, table_ref, sinks_ref):
    out = []
    for h in range(N_HEADS):
        acc = jnp.full(ids.shape, MASK_VALUE, F32)
        for bucket in range(N_BUCKETS):
            acc = jnp.where(ids == bucket, table_ref[bucket, h], acc)
        out.append(jnp.where(ids == SINK_ID, sinks_ref[h], acc))
    return out


def _store_stacked_bias(bias_ref, prefix, per_head, q_rows):
    for h, b in enumerate(per_head):
        j, g = divmod(h, GROUP)
        r0 = HEAD_ORDER.index(g) * q_rows
        bias_ref[prefix + (j, slice(r0, r0 + q_rows))] = b


def _run_lengths(n_items, n_runs):
    base, extra = divmod(n_items, n_runs)
    return [base + (1 if i >= n_runs - extra else 0) for i in range(n_runs)]


def _split_columns(width, n_parts):
    size, rem = divmod(width, n_parts)
    assert rem == 0 and size % MXU_TILE == 0
    return [(i * size, (i + 1) * size) for i in range(n_parts)]


def _stack_heads(q_ref, qsw_ref, rows, j):
    c0 = j * GROUP * HEAD_DIM
    return jnp.concatenate([q_ref[rows, c0:c0 + LANES], q_ref[rows, c0 + LANES:c0 + 2 * LANES],
                            qsw_ref[rows, c0:c0 + LANES], qsw_ref[rows, c0 + LANES:c0 + 2 * LANES]], axis=0)


def _attend_scores(q_heads, key_parts, bias):
    s = _dot_nt(q_heads, jnp.concatenate(key_parts, axis=0)) + bias
    return jnp.exp(s - jnp.max(s, axis=-1, keepdims=True)).astype(BF16)


def _attend_values(probs, value_parts, q_rows):
    o = _dot(probs, jnp.concatenate(value_parts, axis=0))
    o = o[:, :LANES] / o[:, LANES:]
    even, odd = o[:2 * q_rows], pltpu.roll(o[2 * q_rows:], HEAD_DIM, axis=1)
    pairs = even + odd
    return pairs[:q_rows], pairs[q_rows:]


def _aux_kernel(xs_ref, meta_ref, g_mix_ref, w_in_ref, conv_w_ref, state_ref, ck_ref, cv_ref,
                cmk_ref, cmv_ref, ids_ref, table_ref, sinks_ref, g_conv_ref, g_attn_ref, w_out_ref,
                g_mlp_ref, w_up_hbm_ref, w_down_hbm_ref, g_final_ref,
                ys_ref, sk_ref, sv_ref, sconv_ref, mk2_ref, mv2_ref, mk_ref, mv_ref, muc_ref,
                ucb_ref, q_ref, qsw_ref, ya_ref, yc_ref, bias_ref, w_up_ref, w_down_ref, dma_sem,
                *, n_batch, seq, n_keys):
    up_copy = pltpu.make_async_copy(w_up_hbm_ref, w_up_ref, dma_sem.at[0])
    down_copy = pltpu.make_async_copy(w_down_hbm_ref, w_down_ref, dma_sem.at[1])
    up_copy.start()
    down_copy.start()

    g_mix = g_mix_ref[...]
    w_in = w_in_ref[...]
    tail = jnp.zeros((TAIL_ROWS, LANES), F32)

    mproj = _dot(_rms(meta_ref[...], g_mix).astype(BF16), w_in)
    mk = mproj[:, OFF_K:OFF_K + KV_DIM]
    mv = mproj[:, OFF_V:OFF_V + KV_DIM]
    mk_ref[...] = mk
    mv_ref[...] = mv
    muc_ref[...] = mproj[:, OFF_C:OFF_C + CONV_DIM] * mproj[:, OFF_U:OFF_U + CONV_DIM]
    for j, layout in enumerate(_head_layouts(jnp.concatenate([mk, tail], axis=0))):
        mk2_ref[j] = layout
    for j, layout in enumerate(_head_layouts(jnp.concatenate([mv, tail], axis=0))):
        mv2_ref[j] = jnp.concatenate([layout, _denominator_lanes(META_ROWS, N_META + 1)], axis=1)

    _store_stacked_bias(bias_ref, (), _build_bias(ids_ref[...], table_ref, sinks_ref), seq)

    xs = xs_ref[...]
    proj = _dot(_rms(xs, g_mix).astype(BF16), w_in)
    uc = proj[:, OFF_C:OFF_C + CONV_DIM] * proj[:, OFF_U:OFF_U + CONV_DIM]
    sk_ref[...] = proj[:, OFF_K:OFF_K + KV_DIM]
    sv_ref[...] = proj[:, OFF_V:OFF_V + KV_DIM]
    q = proj[:, OFF_Q:OFF_Q + Q_DIM] * (HEAD_DIM ** -0.5)
    q_ref[...] = q.astype(BF16)
    qsw_ref[...] = _swap_lane_halves(q).astype(BF16)
    yc_ref[...] = proj[:, OFF_B:OFF_B + CONV_DIM]
    w0 = conv_w_ref[0:1, :]
    w1 = conv_w_ref[1:2, :]
    w2 = conv_w_ref[2:3, :]
    head = SUBLANES
    for b in range(n_batch):
        ucb_ref[b, head - 2:head, :] = state_ref[b]
        ucb_ref[b, head:head + seq, :] = uc[b * seq:(b + 1) * seq]
        sconv_ref[b] = uc[(b + 1) * seq - 2:(b + 1) * seq]
    den_lanes = _denominator_lanes(n_keys + TAIL_ROWS, n_keys + 1)

    for b in range(n_batch):
        rows = slice(b * seq, (b + 1) * seq)
        conv = (w0 * ucb_ref[b, head - 2:head - 2 + seq, :] + w1 * ucb_ref[b, head - 1:head - 1 + seq, :]
                + w2 * ucb_ref[b, head:head + seq, :])
        yc_ref[rows, :] = yc_ref[rows, :] * conv
        keys = _head_layouts(jnp.concatenate([cmk_ref[b], ck_ref[b], sk_ref[rows, :], tail], axis=0))
        vals = _head_layouts(jnp.concatenate([cmv_ref[b], cv_ref[b], sv_ref[rows, :], tail], axis=0))
        for j in range(N_KV_HEADS):
            c0 = j * GROUP * HEAD_DIM
            probs = _attend_scores(_stack_heads(q_ref, qsw_ref, rows, j), [keys[j]], bias_ref[j])
            pair0, pair1 = _attend_values(probs, [jnp.concatenate([vals[j], den_lanes], axis=1)], seq)
            ya_ref[rows, c0:c0 + LANES] = pair0
            ya_ref[rows, c0 + LANES:c0 + 2 * LANES] = pair1

    yc_n = _rms(yc_ref[...], g_conv_ref[...]).astype(BF16)
    ya_n = _rms(ya_ref[...], g_attn_ref[...]).astype(BF16)
    h1 = xs + _dot(yc_n, w_out_ref[0:CONV_DIM, :]) + _dot(ya_n, w_out_ref[CONV_DIM:, :])

    up_copy.wait()
    u = jnp.maximum(_dot(_rms(h1, g_mlp_ref[...]).astype(BF16), w_up_ref[...]), 0.0)
    down_copy.wait()
    h2 = h1 + _dot((u * u).astype(BF16), w_down_ref[...])
    ys_ref[...] = _rms(h2, g_final_ref[...])


def _smem():
    return pl.BlockSpec(memory_space=pltpu.SMEM)


def _aux_call(xs, meta, g_mix, w_in, conv_w, state, ck, cv, cmk, cmv, ids, table, sinks, g_conv, g_attn, w_out,
              g_mlp, w_up, w_down, g_final, *, n_batch, seq):
    n_tok = n_batch * seq
    n_keys = ids.shape[1] - TAIL_ROWS
    out_shape = (
        jax.ShapeDtypeStruct((n_tok, D_MODEL), F32),
        jax.ShapeDtypeStruct((n_tok, KV_DIM), F32),
        jax.ShapeDtypeStruct((n_tok, KV_DIM), F32),
        jax.ShapeDtypeStruct((n_batch, CONV_W - 1, CONV_DIM), F32),
        jax.ShapeDtypeStruct((N_KV_HEADS, META_ROWS, LANES), BF16),
        jax.ShapeDtypeStruct((N_KV_HEADS, META_ROWS, 2 * LANES), BF16),
        jax.ShapeDtypeStruct((N_META, KV_DIM), F32),
        jax.ShapeDtypeStruct((N_META, KV_DIM), F32),
        jax.ShapeDtypeStruct((N_META, CONV_DIM), F32),
    )
    vmem = pl.BlockSpec(memory_space=pltpu.VMEM)
    hbm = pl.BlockSpec(memory_space=pl.ANY)
    return pl.pallas_call(
        functools.partial(_aux_kernel, n_batch=n_batch, seq=seq, n_keys=n_keys),
        in_specs=[vmem] * 11 + [_smem(), _smem()] + [vmem] * 4 + [hbm, hbm, vmem],
        out_shape=out_shape,
        scratch_shapes=[
            pltpu.VMEM((n_batch, SUBLANES + seq, CONV_DIM), F32),
            pltpu.VMEM((n_tok, Q_DIM), BF16),
            pltpu.VMEM((n_tok, Q_DIM), BF16),
            pltpu.VMEM((n_tok, Q_DIM), F32),
            pltpu.VMEM((n_tok, CONV_DIM), F32),
            pltpu.VMEM((N_KV_HEADS, GROUP * seq, n_keys + TAIL_ROWS), F32),
            pltpu.VMEM(w_up.shape, w_up.dtype),
            pltpu.VMEM(w_down.shape, w_down.dtype),
            pltpu.SemaphoreType.DMA((2,)),
        ],
        compiler_params=pltpu.CompilerParams(vmem_limit_bytes=VMEM_LIMIT_BYTES),
        name="aux_sample_layer",
    )(xs, meta, g_mix, w_in, conv_w, state, ck, cv, cmk, cmv, ids, table, sinks, g_conv, g_attn, w_out,
      g_mlp, w_up, w_down, g_final)


def _layer_kernel(xnext_ref, xprev_ref, g_mix_ref, w_in_ref, conv_w_ref, mk2_ref, mv2_ref, muc_ref, ids_ref,
                  table_ref, sinks_ref, g_conv_ref, g_attn_ref, w_out_ref, g_mlp_ref, w_up_ref, w_down_ref,
                  g_final_ref,
                  y_ref, pk_ref, pv_ref, pc_ref,
                  ucb_ref, kb_ref, vb_ref, q_ref, qsw_ref, ya_ref, bias_ref, yn_ref, hn_ref, xn_ref, proj_ref,
                  *, tile, n_tiles, n_total):
    s = pl.program_id(0)

    @pl.when(s == 0)
    def _():
        for cls in range(WIN_CHUNKS + 1):
            _store_stacked_bias(bias_ref, (cls,), _build_bias(ids_ref[cls], table_ref, sinks_ref), CHUNK)
        ucb_ref[...] = jnp.zeros(ucb_ref.shape, F32)
        kb_ref[...] = jnp.zeros(kb_ref.shape, BF16)
        ones = _denominator_lanes(WINDOW + tile, WINDOW + tile)
        for sub in range(TILES_PER_STEP):
            for j in range(N_KV_HEADS):
                vb_ref[sub, j] = jnp.concatenate([jnp.zeros_like(ones), ones], axis=1)
        yn_ref[...] = jnp.zeros(yn_ref.shape, BF16)
        proj_ref[...] = jnp.zeros(proj_ref.shape, F32)

    for sub in range(TILES_PER_STEP):
        rows = pl.ds(sub * tile, tile)
        before = (sub - 1) % TILES_PER_STEP
        _layer_tile(s * TILES_PER_STEP - 1 + sub, sub == 0,
                    xnext_ref.at[rows], xprev_ref.at[rows], g_mix_ref, w_in_ref, conv_w_ref, mk2_ref, mv2_ref,
                    muc_ref, g_conv_ref, g_attn_ref, w_out_ref, g_mlp_ref, w_up_ref, w_down_ref, g_final_ref,
                    y_ref.at[rows], pk_ref, pv_ref, pc_ref,
                    ucb_ref.at[sub], kb_ref.at[sub], vb_ref.at[sub],
                    ucb_ref.at[before], kb_ref.at[before], vb_ref.at[before],
                    q_ref, qsw_ref, ya_ref, bias_ref, yn_ref, hn_ref, xn_ref, proj_ref,
                    tile=tile, n_tiles=n_tiles, n_total=n_total)


def _layer_tile(g, write_tails, xnext_ref, xprev_ref, g_mix_ref, w_in_ref, conv_w_ref, mk2_ref, mv2_ref,
                muc_ref, g_conv_ref, g_attn_ref, w_out_ref, g_mlp_ref, w_up_ref, w_down_ref, g_final_ref,
                y_ref, pk_ref, pv_ref, pc_ref,
                ucb_ref, kb_ref, vb_ref, ucb_before_ref, kb_before_ref, vb_before_ref,
                q_ref, qsw_ref, ya_ref, bias_ref, yn_ref, hn_ref, xn_ref, proj_ref,
                *, tile, n_tiles, n_total):
    t = lax.rem(jnp.clip(g, 0, n_total - 1), n_tiles)
    first = t == 0
    head = SUBLANES
    chunks = tile // CHUNK
    groups = chunks // CHUNK_GROUP
    pieces = _split_columns(D_FF, MLP_PIECES)
    hidden = {}

    def mlp_up(idx):
        n0, n1 = pieces[idx]
        u = jnp.maximum(_dot(hn_ref[...], w_up_ref[:, n0:n1]), 0.0)
        hidden[idx] = (u * u).astype(BF16)

    def mlp_down(acc, idx):
        n0, n1 = pieces[idx]
        return acc + _dot(hidden.pop(idx), w_down_ref[n0:n1, :])

    def scores(i):
        cls = jnp.where(first, i, WIN_CHUNKS) if i < WIN_CHUNKS else WIN_CHUNKS
        rows = slice(i * CHUNK, (i + 1) * CHUNK)
        band = slice(i * CHUNK, i * CHUNK + BAND)
        return [_attend_scores(_stack_heads(q_ref, qsw_ref, rows, j), [kb_ref[j, band, :], mk2_ref[j]],
                               bias_ref[cls, j]) for j in range(N_KV_HEADS)]

    def values(i, probs):
        rows = slice(i * CHUNK, (i + 1) * CHUNK)
        band = slice(i * CHUNK, i * CHUNK + BAND)
        for j in range(N_KV_HEADS):
            c0 = j * GROUP * HEAD_DIM
            pair0, pair1 = _attend_values(probs[j], [vb_ref[j, band, :], mv2_ref[j]], CHUNK)
            ya_ref[rows, c0:c0 + LANES] = pair0
            ya_ref[rows, c0 + LANES:c0 + 2 * LANES] = pair1

    h1 = xprev_ref[...] + _dot(yn_ref[...], w_out_ref[...])
    hn_ref[...] = _rms(h1, g_mlp_ref[...]).astype(BF16)

    xn_ref[...] = _rms(xnext_ref[...], g_mix_ref[...]).astype(BF16)

    uc = proj_ref[:, OFF_C:OFF_C + CONV_DIM] * proj_ref[:, OFF_U:OFF_U + CONV_DIM]
    k = proj_ref[:, OFF_K:OFF_K + KV_DIM]
    v = proj_ref[:, OFF_V:OFF_V + KV_DIM]

    ucb_ref[head - 2:head, :] = jnp.where(first, muc_ref[N_META - 2:N_META, :],
                                          ucb_before_ref[head + tile - 2:head + tile, :])
    zeros = jnp.zeros((N_KV_HEADS, WINDOW, LANES), BF16)
    kb_ref[:, 0:WINDOW, :] = jnp.where(first, zeros, kb_before_ref[:, tile:tile + WINDOW, :])
    vb_ref[:, 0:WINDOW, 0:LANES] = jnp.where(first, zeros, vb_before_ref[:, tile:tile + WINDOW, 0:LANES])

    ucb_ref[head:head + tile, :] = uc
    for j, layout in enumerate(_head_layouts(k)):
        kb_ref[j, WINDOW:WINDOW + tile, :] = layout
    for j, layout in enumerate(_head_layouts(v)):
        vb_ref[j, WINDOW:WINDOW + tile, 0:LANES] = layout
    q = proj_ref[:, OFF_Q:OFF_Q + Q_DIM] * (HEAD_DIM ** -0.5)
    q_ref[...] = q.astype(BF16)
    qsw_ref[...] = _swap_lane_halves(q).astype(BF16)

    if write_tails:
        pk_ref[...] = k[tile - WINDOW:tile]
        pv_ref[...] = v[tile - WINDOW:tile]
        pc_ref[...] = uc[tile - (CONV_W - 1):tile]

    conv = (conv_w_ref[0:1, :] * ucb_ref[head - 2:head - 2 + tile, :]
            + conv_w_ref[1:2, :] * ucb_ref[head - 1:head - 1 + tile, :]
            + conv_w_ref[2:3, :] * uc)
    yc = proj_ref[:, OFF_B:OFF_B + CONV_DIM] * conv
    yn_ref[:, 0:CONV_DIM] = _rms(yc, g_conv_ref[...]).astype(BF16)

    proj_ref[:, 0:OFF_Q] = _dot(xn_ref[...], w_in_ref[:, 0:OFF_Q])

    mlp_items = [("up", 0)]
    for idx in range(1, MLP_PIECES):
        mlp_items += [("up", idx), ("down", idx - 1)]
    mlp_items.append(("down", MLP_PIECES - 1))
    runs = _run_lengths(len(mlp_items), groups + 1)

    group_of = lambda g: range(g * CHUNK_GROUP, (g + 1) * CHUNK_GROUP)
    probs = [scores(i) for i in group_of(0)]
    h2 = h1
    items = iter(mlp_items)
    for g in range(groups + 1):
        for _ in range(runs[g]):
            kind, idx = next(items)
            if kind == "up":
                mlp_up(idx)
            else:
                h2 = mlp_down(h2, idx)
        if g < groups:
            for i, p in zip(group_of(g), probs):
                values(i, p)
        if g + 1 < groups:
            probs = [scores(i) for i in group_of(g + 1)]
    y_ref[...] = _rms(h2, g_final_ref[...])
    yn_ref[:, CONV_DIM:] = _rms(ya_ref[...], g_attn_ref[...]).astype(BF16)

    proj_ref[:, OFF_Q:] = _dot(xn_ref[...], w_in_ref[:, OFF_Q:])


def _resident(shape):
    return pl.BlockSpec(shape, lambda *_: (0,) * len(shape), pipeline_mode=pl.Buffered(1))


def _layer_call(x, g_mix, w_in, conv_w, mk2, mv2, muc, ids, table, sinks, g_conv, g_attn, w_out,
                g_mlp, w_up, w_down, g_final):
    n_batch, seq, _ = x.shape
    tile = LAYER_TILE
    block = tile * TILES_PER_STEP
    n_tiles = seq // tile
    n_total = n_batch * n_tiles
    n_blocks = n_total // TILES_PER_STEP
    blocks_per_row = seq // block
    assert seq % block == 0 and tile % (CHUNK * CHUNK_GROUP) == 0 and tile >= WINDOW
    n_keys = BAND + META_ROWS
    out_shape = (
        jax.ShapeDtypeStruct((n_batch, seq, D_MODEL), F32),
        jax.ShapeDtypeStruct((n_batch, WINDOW, KV_DIM), F32),
        jax.ShapeDtypeStruct((n_batch, WINDOW, KV_DIM), F32),
        jax.ShapeDtypeStruct((n_batch, CONV_W - 1, CONV_DIM), F32),
    )

    def x_block(offset):
        def index_map(s):
            b = jnp.clip(s + offset, 0, n_blocks - 1)
            return b // blocks_per_row, b % blocks_per_row, 0
        return index_map

    def tails_block(s):
        first_tile = jnp.clip(s * TILES_PER_STEP - 1, 0, n_total - 1)
        return first_tile // n_tiles, 0, 0

    per_batch = lambda shape: pl.BlockSpec((None,) + shape, tails_block)
    operands = (x, x, g_mix, w_in, conv_w, mk2, mv2, muc, ids, table, sinks, g_conv, g_attn, w_out,
                g_mlp, w_up, w_down, g_final)
    in_specs = [pl.BlockSpec((None, block, D_MODEL), x_block(0)), pl.BlockSpec((None, block, D_MODEL), x_block(-1))]
    in_specs += [_smem() if a is table or a is sinks else _resident(a.shape) for a in operands[2:]]
    return pl.pallas_call(
        functools.partial(_layer_kernel, tile=tile, n_tiles=n_tiles, n_total=n_total),
        grid=(n_blocks + 1,),
        in_specs=in_specs,
        out_specs=(
            pl.BlockSpec((None, block, D_MODEL), x_block(-1)),
            per_batch((WINDOW, KV_DIM)), per_batch((WINDOW, KV_DIM)), per_batch((CONV_W - 1, CONV_DIM)),
        ),
        out_shape=out_shape,
        scratch_shapes=[
            pltpu.VMEM((TILES_PER_STEP, SUBLANES + tile, CONV_DIM), F32),
            pltpu.VMEM((TILES_PER_STEP, N_KV_HEADS, WINDOW + tile, LANES), BF16),
            pltpu.VMEM((TILES_PER_STEP, N_KV_HEADS, WINDOW + tile, 2 * LANES), BF16),
            pltpu.VMEM((tile, Q_DIM), BF16),
            pltpu.VMEM((tile, Q_DIM), BF16),
            pltpu.VMEM((tile, Q_DIM), F32),
            pltpu.VMEM((WIN_CHUNKS + 1, N_KV_HEADS, GROUP * CHUNK, n_keys), F32),
            pltpu.VMEM((tile, CONV_DIM + Q_DIM), BF16),
            pltpu.VMEM((tile, D_MODEL), BF16),
            pltpu.VMEM((tile, D_MODEL), BF16),
            pltpu.VMEM((tile, IN_DIM), F32),
        ],
        compiler_params=pltpu.CompilerParams(
            dimension_semantics=("arbitrary",), vmem_limit_bytes=VMEM_LIMIT_BYTES),
        name="prompt_layer",
    )(*operands)


def _t5_bucket(rp):
    nb = N_BUCKETS // 2
    max_exact = nb // 2
    ret = jnp.where(rp > 0, nb, 0)
    n = jnp.abs(rp)
    nf = jnp.maximum(n, 1).astype(F32)
    large = max_exact + (jnp.log(nf / max_exact) / math.log(MAX_DISTANCE / max_exact)
                         * (nb - max_exact)).astype(jnp.int32)
    large = jnp.minimum(large, nb - 1)
    return ret + jnp.where(n < max_exact, n, large)


def _bucket_ids(q_pos, k_pos, valid=None):
    ids = _t5_bucket(k_pos[None, :] - q_pos[:, None])
    if valid is not None:
        ids = jnp.where(valid[None, :], ids, MASKED_ID)
    n_q = q_pos.shape[0]
    tail = jnp.concatenate([jnp.full((n_q, 1), SINK_ID, jnp.int32),
                            jnp.full((n_q, TAIL_ROWS - 1), MASKED_ID, jnp.int32)], axis=1)
    return jnp.concatenate([ids, tail], axis=1)


def _prompt_bucket_ids():
    qi = jnp.arange(CHUNK, dtype=jnp.int32)
    r = jnp.arange(BAND, dtype=jnp.int32)
    pm = jnp.arange(N_META, dtype=jnp.int32)
    out = []
    for c in range(WIN_CHUNKS + 1):
        frame_k = c * CHUNK - WIN_CHUNKS * CHUNK + r
        k_pos = jnp.concatenate([N_META + frame_k, pm])
        valid = jnp.concatenate([frame_k >= 0, jnp.ones((N_META,), bool)])
        out.append(_bucket_ids(N_META + c * CHUNK + qi, k_pos, valid))
    return jnp.stack(out)


def _sample_bucket_ids(n_win, seq):
    s = jnp.arange(seq, dtype=jnp.int32)
    k_pos = jnp.concatenate([jnp.arange(N_META, dtype=jnp.int32),
                             N_META + PAST_LEN - n_win + jnp.arange(n_win, dtype=jnp.int32),
                             N_META + PAST_LEN + s])
    return _bucket_ids(N_META + PAST_LEN + s, k_pos)


def kernel(x_prompt, x_sample, cache_k, cache_v, cache_meta_k, cache_meta_v, state_conv, meta_tokens,
           norm_mix, w_in, conv_w, attn_sinks, rel_bias_table, norm_conv_out, norm_attn_out, w_out,
           norm_mlp, w_up, w_down, norm_final):
    n_batch, seq, _ = x_prompt.shape
    s_batch, s_seq, _ = x_sample.shape
    n_win = cache_k.shape[2]
    row = lambda a: a.reshape(1, -1)

    w_in_b = w_in[0].astype(BF16)
    w_out_b = w_out[0].astype(BF16)
    w_up_b = w_up[0].astype(BF16)
    w_down_b = w_down[0].astype(BF16)
    g_mix, g_conv, g_attn = row(norm_mix[0]), row(norm_conv_out[0]), row(norm_attn_out[0])
    g_mlp, g_final = row(norm_mlp[0]), row(norm_final)
    table = rel_bias_table.astype(F32)
    sinks = attn_sinks[0].astype(F32)

    y_sample, s_k, s_v, s_conv, mk2, mv2, mk, mv, muc = _aux_call(
        x_sample.reshape(s_batch * s_seq, D_MODEL), meta_tokens, g_mix, w_in_b, conv_w[0], state_conv[0],
        cache_k[0].reshape(s_batch, n_win, KV_DIM), cache_v[0].reshape(s_batch, n_win, KV_DIM),
        cache_meta_k[0].reshape(s_batch, N_META, KV_DIM), cache_meta_v[0].reshape(s_batch, N_META, KV_DIM),
        _sample_bucket_ids(n_win, s_seq), table, sinks, g_conv, g_attn, w_out_b,
        g_mlp, w_up_b, w_down_b, g_final, n_batch=s_batch, seq=s_seq)

    y_prompt, p_k, p_v, p_conv = _layer_call(
        x_prompt, g_mix, w_in_b, conv_w[0], mk2, mv2, muc, _prompt_bucket_ids(), table, sinks,
        g_conv, g_attn, w_out_b, g_mlp, w_up_b, w_down_b, g_final)

    kv_shape = lambda a, n, length: a.reshape(1, n, length, N_KV_HEADS, HEAD_DIM)
    meta_shape = (1, n_batch, N_META, N_KV_HEADS, HEAD_DIM)
    return (
        y_prompt,
        y_sample.reshape(s_batch, s_seq, D_MODEL),
        kv_shape(p_k, n_batch, WINDOW), kv_shape(p_v, n_batch, WINDOW),
        jnp.broadcast_to(mk.reshape(1, 1, N_META, N_KV_HEADS, HEAD_DIM), meta_shape),
        jnp.broadcast_to(mv.reshape(1, 1, N_META, N_KV_HEADS, HEAD_DIM), meta_shape),
        p_conv[None],
        kv_shape(s_k, s_batch, s_seq), kv_shape(s_v, s_batch, s_seq),
        s_conv[None],
    )
```

```python
import functools
import math

import jax
import jax.numpy as jnp
from jax import lax
from jax.experimental import pallas as pl
from jax.experimental.pallas import tpu as pltpu

D_MODEL = 1024
CHUNK = 64
N_META = 16
CONV_DIM = 512
CONV_W = 3
N_HEADS = 8
N_KV_HEADS = 2
HEAD_DIM = 64
GROUP = N_HEADS // N_KV_HEADS
Q_DIM = N_HEADS * HEAD_DIM
KV_DIM = N_KV_HEADS * HEAD_DIM
IN_DIM = 3 * CONV_DIM + Q_DIM + 2 * KV_DIM
WINDOW = 128
WIN_CHUNKS = WINDOW // CHUNK
BAND = (WIN_CHUNKS + 1) * CHUNK
N_BUCKETS = 32
MAX_DISTANCE = 128
D_FF = 4 * D_MODEL
EPS = 1e-6
PAST_LEN = 4096

OFF_B, OFF_C, OFF_U = 0, CONV_DIM, 2 * CONV_DIM
OFF_Q = 3 * CONV_DIM
OFF_K = OFF_Q + Q_DIM
OFF_V = OFF_K + KV_DIM

LANES = 128
SUBLANES = 8
BF16_ROWS = 16
MXU_TILE = 256
VMEM_LIMIT_BYTES = 56 * 1024 * 1024
MASK_VALUE = float("-inf")

LAYER_TILE = 256
TILES_PER_STEP = 2
CHUNK_GROUP = 2
MLP_PIECES = 4
TAIL_ROWS = BF16_ROWS
META_ROWS = N_META + TAIL_ROWS
MASKED_ID = -1
SINK_ID = -2
HEAD_ORDER = (0, 2, 1, 3)

BF16 = jnp.bfloat16
F32 = jnp.float32


def _rms(x, g):
    ms = jnp.mean(x * x, axis=-1, keepdims=True)
    return x * lax.rsqrt(ms + EPS) * g


def _dot(a, b):
    return jnp.dot(a, b, preferred_element_type=F32)


def _dot_nt(a, b):
    return lax.dot_general(a, b, (((1,), (1,)), ((), ())), preferred_element_type=F32)


def _swap_lane_halves(x):
    pairs = [pltpu.roll(x[:, c:c + LANES], HEAD_DIM, axis=1) for c in range(0, x.shape[1], LANES)]
    return jnp.concatenate(pairs, axis=1)


def _head_layouts(kv):
    lane = lax.broadcasted_iota(jnp.int32, kv.shape, 1)
    low = lane < HEAD_DIM
    zero = jnp.zeros_like(kv)
    return (jnp.where(low, kv, zero).astype(BF16),
            jnp.where(low, pltpu.roll(kv, HEAD_DIM, axis=1), zero).astype(BF16))


def _denominator_lanes(n_rows, n_counted):
    row = lax.broadcasted_iota(jnp.int32, (n_rows, LANES), 0)
    return jnp.where(row < n_counted, 1.0, 0.0).astype(BF16)


def _build_bias(ids, table_ref, sinks_ref):
    out = []
    for h in range(N_HEADS):
        acc = jnp.full(ids.shape, MASK_VALUE, F32)
        for bucket in range(N_BUCKETS):
            acc = jnp.where(ids == bucket, table_ref[bucket, h], acc)
        out.append(jnp.where(ids == SINK_ID, sinks_ref[h], acc))
    return out


def _store_stacked_bias(bias_ref, prefix, per_head, q_rows):
    for h, b in enumerate(per_head):
        j, g = divmod(h, GROUP)
        r0 = HEAD_ORDER.index(g) * q_rows
        bias_ref[prefix + (j, slice(r0, r0 + q_rows))] = b


def _run_lengths(n_items, n_runs):
    base, extra = divmod(n_items, n_runs)
    return [base + (1 if i >= n_runs - extra else 0) for i in range(n_runs)]


def _split_columns(width, n_parts):
    size, rem = divmod(width, n_parts)
    assert rem == 0 and size % MXU_TILE == 0
    return [(i * size, (i + 1) * size) for i in range(n_parts)]


def _stack_heads(q_ref, qsw_ref, rows, j):
    c0 = j * GROUP * HEAD_DIM
    return jnp.concatenate([q_ref[rows, c0:c0 + LANES], q_ref[rows, c0 + LANES:c0 + 2 * LANES],
                            qsw_ref[rows, c0:c0 + LANES], qsw_ref[rows, c0 + LANES:c0 + 2 * LANES]], axis=0)


def _attend_scores(q_heads, key_parts, bias):
    s = _dot_nt(q_heads, jnp.concatenate(key_parts, axis=0)) + bias
    return jnp.exp(s - jnp.max(s, axis=-1, keepdims=True)).astype(BF16)


def _attend_values(probs, value_parts, q_rows):
    o = _dot(probs, jnp.concatenate(value_parts, axis=0))
    o = o[:, :LANES] / o[:, LANES:]
    even, odd = o[:2 * q_rows], pltpu.roll(o[2 * q_rows:], HEAD_DIM, axis=1)
    pairs = even + odd
    return pairs[:q_rows], pairs[q_rows:]


def _aux_kernel(xs_ref, meta_ref, g_mix_ref, w_in_ref, conv_w_ref, state_ref, ck_ref, cv_ref,
                cmk_ref, cmv_ref, ids_ref, table_ref, sinks_ref, g_conv_ref, g_attn_ref, w_out_ref,
                g_mlp_ref, w_up_hbm_ref, w_down_hbm_ref, g_final_ref,
                ys_ref, sk_ref, sv_ref, sconv_ref, mk2_ref, mv2_ref, mk_ref, mv_ref, muc_ref,
                ucb_ref, q_ref, qsw_ref, ya_ref, yc_ref, bias_ref, w_up_ref, w_down_ref, dma_sem,
                *, n_batch, seq, n_keys):
    up_copy = pltpu.make_async_copy(w_up_hbm_ref, w_up_ref, dma_sem.at[0])
    down_copy = pltpu.make_async_copy(w_down_hbm_ref, w_down_ref, dma_sem.at[1])
    up_copy.start()
    down_copy.start()

    g_mix = g_mix_ref[...]
    w_in = w_in_ref[...]
    tail = jnp.zeros((TAIL_ROWS, LANES), F32)

    mproj = _dot(_rms(meta_ref[...], g_mix).astype(BF16), w_in)
    mk = mproj[:, OFF_K:OFF_K + KV_DIM]
    mv = mproj[:, OFF_V:OFF_V + KV_DIM]
    mk_ref[...] = mk
    mv_ref[...] = mv
    muc_ref[...] = mproj[:, OFF_C:OFF_C + CONV_DIM] * mproj[:, OFF_U:OFF_U + CONV_DIM]
    for j, layout in enumerate(_head_layouts(jnp.concatenate([mk, tail], axis=0))):
        mk2_ref[j] = layout
    for j, layout in enumerate(_head_layouts(jnp.concatenate([mv, tail], axis=0))):
        mv2_ref[j] = jnp.concatenate([layout, _denominator_lanes(META_ROWS, N_META + 1)], axis=1)

    _store_stacked_bias(bias_ref, (), _build_bias(ids_ref[...], table_ref, sinks_ref), seq)

    xs = xs_ref[...]
    proj = _dot(_rms(xs, g_mix).astype(BF16), w_in)
    uc = proj[:, OFF_C:OFF_C + CONV_DIM] * proj[:, OFF_U:OFF_U + CONV_DIM]
    sk_ref[...] = proj[:, OFF_K:OFF_K + KV_DIM]
    sv_ref[...] = proj[:, OFF_V:OFF_V + KV_DIM]
    q = proj[:, OFF_Q:OFF_Q + Q_DIM] * (HEAD_DIM ** -0.5)
    q_ref[...] = q.astype(BF16)
    qsw_ref[...] = _swap_lane_halves(q).astype(BF16)
    yc_ref[...] = proj[:, OFF_B:OFF_B + CONV_DIM]
    w0 = conv_w_ref[0:1, :]
    w1 = conv_w_ref[1:2, :]
    w2 = conv_w_ref[2:3, :]
    head = SUBLANES
    for b in range(n_batch):
        ucb_ref[b, head - 2:head, :] = state_ref[b]
        ucb_ref[b, head:head + seq, :] = uc[b * seq:(b + 1) * seq]
        sconv_ref[b] = uc[(b + 1) * seq - 2:(b + 1) * seq]
    den_lanes = _denominator_lanes(n_keys + TAIL_ROWS, n_keys + 1)

    for b in range(n_batch):
        rows = slice(b * seq, (b + 1) * seq)
        conv = (w0 * ucb_ref[b, head - 2:head - 2 + seq, :] + w1 * ucb_ref[b, head - 1:head - 1 + seq, :]
                + w2 * ucb_ref[b, head:head + seq, :])
        yc_ref[rows, :] = yc_ref[rows, :] * conv
        keys = _head_layouts(jnp.concatenate([cmk_ref[b], ck_ref[b], sk_ref[rows, :], tail], axis=0))
        vals = _head_layouts(jnp.concatenate([cmv_ref[b], cv_ref[b], sv_ref[rows, :], tail], axis=0))
        for j in range(N_KV_HEADS):
            c0 = j * GROUP * HEAD_DIM
            probs = _attend_scores(_stack_heads(q_ref, qsw_ref, rows, j), [keys[j]], bias_ref[j])
            pair0, pair1 = _attend_values(probs, [jnp.concatenate([vals[j], den_lanes], axis=1)], seq)
            ya_ref[rows, c0:c0 + LANES] = pair0
            ya_ref[rows, c0 + LANES:c0 + 2 * LANES] = pair1

    yc_n = _rms(yc_ref[...], g_conv_ref[...]).astype(BF16)
    ya_n = _rms(ya_ref[...], g_attn_ref[...]).astype(BF16)
    h1 = xs + _dot(yc_n, w_out_ref[0:CONV_DIM, :]) + _dot(ya_n, w_out_ref[CONV_DIM:, :])

    up_copy.wait()
    u = jnp.maximum(_dot(_rms(h1, g_mlp_ref[...]).astype(BF16), w_up_ref[...]), 0.0)
    down_copy.wait()
    h2 = h1 + _dot((u * u).astype(BF16), w_down_ref[...])
    ys_ref[...] = _rms(h2, g_final_ref[...])


def _smem():
    return pl.BlockSpec(memory_space=pltpu.SMEM)


def _aux_call(xs, meta, g_mix, w_in, conv_w, state, ck, cv, cmk, cmv, ids, table, sinks, g_conv, g_attn, w_out,
              g_mlp, w_up, w_down, g_final, *, n_batch, seq):
    n_tok = n_batch * seq
    n_keys = ids.shape[1] - TAIL_ROWS
    out_shape = (
        jax.ShapeDtypeStruct((n_tok, D_MODEL), F32),
        jax.ShapeDtypeStruct((n_tok, KV_DIM), F32),
        jax.ShapeDtypeStruct((n_tok, KV_DIM), F32),
        jax.ShapeDtypeStruct((n_batch, CONV_W - 1, CONV_DIM), F32),
        jax.ShapeDtypeStruct((N_KV_HEADS, META_ROWS, LANES), BF16),
        jax.ShapeDtypeStruct((N_KV_HEADS, META_ROWS, 2 * LANES), BF16),
        jax.ShapeDtypeStruct((N_META, KV_DIM), F32),
        jax.ShapeDtypeStruct((N_META, KV_DIM), F32),
        jax.ShapeDtypeStruct((N_META, CONV_DIM), F32),
    )
    vmem = pl.BlockSpec(memory_space=pltpu.VMEM)
    hbm = pl.BlockSpec(memory_space=pl.ANY)
    return pl.pallas_call(
        functools.partial(_aux_kernel, n_batch=n_batch, seq=seq, n_keys=n_keys),
        in_specs=[vmem] * 11 + [_smem(), _smem()] + [vmem] * 4 + [hbm, hbm, vmem],
        out_shape=out_shape,
        scratch_shapes=[
            pltpu.VMEM((n_batch, SUBLANES + seq, CONV_DIM), F32),
            pltpu.VMEM((n_tok, Q_DIM), BF16),
            pltpu.VMEM((n_tok, Q_DIM), BF16),
            pltpu.VMEM((n_tok, Q_DIM), F32),
            pltpu.VMEM((n_tok, CONV_DIM), F32),
            pltpu.VMEM((N_KV_HEADS, GROUP * seq, n_keys + TAIL_ROWS), F32),
            pltpu.VMEM(w_up.shape, w_up.dtype),
            pltpu.VMEM(w_down.shape, w_down.dtype),
            pltpu.SemaphoreType.DMA((2,)),
        ],
        compiler_params=pltpu.CompilerParams(vmem_limit_bytes=VMEM_LIMIT_BYTES),
        name="aux_sample_layer",
    )(xs, meta, g_mix, w_in, conv_w, state, ck, cv, cmk, cmv, ids, table, sinks, g_conv, g_attn, w_out,
      g_mlp, w_up, w_down, g_final)


def _layer_kernel(xnext_ref, xprev_ref, g_mix_ref, w_in_ref, conv_w_ref, mk2_ref, mv2_ref, muc_ref, ids_ref,
                  table_ref, sinks_ref, g_conv_ref, g_attn_ref, w_out_ref, g_mlp_ref, w_up_ref, w_down_ref,
                  g_final_ref,
                  y_ref, pk_ref, pv_ref, pc_ref,
                  ucb_ref, kb_ref, vb_ref, q_ref, qsw_ref, ya_ref, bias_ref, yn_ref, hn_ref, xn_ref, proj_ref,
                  *, tile, n_tiles, n_total):
    s = pl.program_id(0)

    @pl.when(s == 0)
    def _():
        for cls in range(WIN_CHUNKS + 1):
            _store_stacked_bias(bias_ref, (cls,), _build_bias(ids_ref[cls], table_ref, sinks_ref), CHUNK)
        ucb_ref[...] = jnp.zeros(ucb_ref.shape, F32)
        kb_ref[...] = jnp.zeros(kb_ref.shape, BF16)
        ones = _denominator_lanes(WINDOW + tile, WINDOW + tile)
        for sub in range(TILES_PER_STEP):
            for j in range(N_KV_HEADS):
                vb_ref[sub, j] = jnp.concatenate([jnp.zeros_like(ones), ones], axis=1)
        yn_ref[...] = jnp.zeros(yn_ref.shape, BF16)
        proj_ref[...] = jnp.zeros(proj_ref.shape, F32)

    for sub in range(TILES_PER_STEP):
        rows = pl.ds(sub * tile, tile)
        before = (sub - 1) % TILES_PER_STEP
        _layer_tile(s * TILES_PER_STEP - 1 + sub, sub == 0,
                    xnext_ref.at[rows], xprev_ref.at[rows], g_mix_ref, w_in_ref, conv_w_ref, mk2_ref, mv2_ref,
                    muc_ref, g_conv_ref, g_attn_ref, w_out_ref, g_mlp_ref, w_up_ref, w_down_ref, g_final_ref,
                    y_ref.at[rows], pk_ref, pv_ref, pc_ref,
                    ucb_ref.at[sub], kb_ref.at[sub], vb_ref.at[sub],
                    ucb_ref.at[before], kb_ref.at[before], vb_ref.at[before],
                    q_ref, qsw_ref, ya_ref, bias_ref, yn_ref, hn_ref, xn_ref, proj_ref,
                    tile=tile, n_tiles=n_tiles, n_total=n_total)


def _layer_tile(g, write_tails, xnext_ref, xprev_ref, g_mix_ref, w_in_ref, conv_w_ref, mk2_ref, mv2_ref,
                muc_ref, g_conv_ref, g_attn_ref, w_out_ref, g_mlp_ref, w_up_ref, w_down_ref, g_final_ref,
                y_ref, pk_ref, pv_ref, pc_ref,
                ucb_ref, kb_ref, vb_ref, ucb_before_ref, kb_before_ref, vb_before_ref,
                q_ref, qsw_ref, ya_ref, bias_ref, yn_ref, hn_ref, xn_ref, proj_ref,
                *, tile, n_tiles, n_total):
    t = lax.rem(jnp.clip(g, 0, n_total - 1), n_tiles)
    first = t == 0
    head = SUBLANES
    chunks = tile // CHUNK
    groups = chunks // CHUNK_GROUP
    pieces = _split_columns(D_FF, MLP_PIECES)
    hidden = {}

    def mlp_up(idx):
        n0, n1 = pieces[idx]
        u = jnp.maximum(_dot(hn_ref[...], w_up_ref[:, n0:n1]), 0.0)
        hidden[idx] = (u * u).astype(BF16)

    def mlp_down(acc, idx):
        n0, n1 = pieces[idx]
        return acc + _dot(hidden.pop(idx), w_down_ref[n0:n1, :])

    def scores(i):
        cls = jnp.where(first, i, WIN_CHUNKS) if i < WIN_CHUNKS else WIN_CHUNKS
        rows = slice(i * CHUNK, (i + 1) * CHUNK)
        band = slice(i * CHUNK, i * CHUNK + BAND)
        return [_attend_scores(_stack_heads(q_ref, qsw_ref, rows, j), [kb_ref[j, band, :], mk2_ref[j]],
                               bias_ref[cls, j]) for j in range(N_KV_HEADS)]

    def values(i, probs):
        rows = slice(i * CHUNK, (i + 1) * CHUNK)
        band = slice(i * CHUNK, i * CHUNK + BAND)
        for j in range(N_KV_HEADS):
            c0 = j * GROUP * HEAD_DIM
            pair0, pair1 = _attend_values(probs[j], [vb_ref[j, band, :], mv2_ref[j]], CHUNK)
            ya_ref[rows, c0:c0 + LANES] = pair0
            ya_ref[rows, c0 + LANES:c0 + 2 * LANES] = pair1

    h1 = xprev_ref[...] + _dot(yn_ref[...], w_out_ref[...])
    hn_ref[...] = _rms(h1, g_mlp_ref[...]).astype(BF16)

    xn_ref[...] = _rms(xnext_ref[...], g_mix_ref[...]).astype(BF16)

    uc = proj_ref[:, OFF_C:OFF_C + CONV_DIM] * proj_ref[:, OFF_U:OFF_U + CONV_DIM]
    k = proj_ref[:, OFF_K:OFF_K + KV_DIM]
    v = proj_ref[:, OFF_V:OFF_V + KV_DIM]

    ucb_ref[head - 2:head, :] = jnp.where(first, muc_ref[N_META - 2:N_META, :],
                                          ucb_before_ref[head + tile - 2:head + tile, :])
    zeros = jnp.zeros((N_KV_HEADS, WINDOW, LANES), BF16)
    kb_ref[:, 0:WINDOW, :] = jnp.where(first, zeros, kb_before_ref[:, tile:tile + WINDOW, :])
    vb_ref[:, 0:WINDOW, 0:LANES] = jnp.where(first, zeros, vb_before_ref[:, tile:tile + WINDOW, 0:LANES])

    ucb_ref[head:head + tile, :] = uc
    for j, layout in enumerate(_head_layouts(k)):
        kb_ref[j, WINDOW:WINDOW + tile, :] = layout
    for j, layout in enumerate(_head_layouts(v)):
        vb_ref[j, WINDOW:WINDOW + tile, 0:LANES] = layout
    q = proj_ref[:, OFF_Q:OFF_Q + Q_DIM] * (HEAD_DIM ** -0.5)
    q_ref[...] = q.astype(BF16)
    qsw_ref[...] = _swap_lane_halves(q).astype(BF16)

    if write_tails:
        pk_ref[...] = k[tile - WINDOW:tile]
        pv_ref[...] = v[tile - WINDOW:tile]
        pc_ref[...] = uc[tile - (CONV_W - 1):tile]

    conv = (conv_w_ref[0:1, :] * ucb_ref[head - 2:head - 2 + tile, :]
            + conv_w_ref[1:2, :] * ucb_ref[head - 1:head - 1 + tile, :]
            + conv_w_ref[2:3, :] * uc)
    yc = proj_ref[:, OFF_B:OFF_B + CONV_DIM] * conv
    yn_ref[:, 0:CONV_DIM] = _rms(yc, g_conv_ref[...]).astype(BF16)

    proj_ref[:, 0:OFF_Q] = _dot(xn_ref[...], w_in_ref[:, 0:OFF_Q])

    mlp_items = [("up", 0)]
    for idx in range(1, MLP_PIECES):
        mlp_items += [("up", idx), ("down", idx - 1)]
    mlp_items.append(("down", MLP_PIECES - 1))
    runs = _run_lengths(len(mlp_items), groups + 1)

    group_of = lambda g: range(g * CHUNK_GROUP, (g + 1) * CHUNK_GROUP)
    probs = [scores(i) for i in group_of(0)]
    h2 = h1
    items = iter(mlp_items)
    for g in range(groups + 1):
        for _ in range(runs[g]):
            kind, idx = next(items)
            if kind == "up":
                mlp_up(idx)
            else:
                h2 = mlp_down(h2, idx)
        if g < groups:
            for i, p in zip(group_of(g), probs):
                values(i, p)
        if g + 1 < groups:
            probs = [scores(i) for i in group_of(g + 1)]
    y_ref[...] = _rms(h2, g_final_ref[...])
    yn_ref[:, CONV_DIM:] = _rms(ya_ref[...], g_attn_ref[...]).astype(BF16)

    proj_ref[:, OFF_Q:] = _dot(xn_ref[...], w_in_ref[:, OFF_Q:])


def _resident(shape):
    return pl.BlockSpec(shape, lambda *_: (0,) * len(shape), pipeline_mode=pl.Buffered(1))


def _layer_call(x, g_mix, w_in, conv_w, mk2, mv2, muc, ids, table, sinks, g_conv, g_attn, w_out,
                g_mlp, w_up, w_down, g_final):
    n_batch, seq, _ = x.shape
    tile = LAYER_TILE
    block = tile * TILES_PER_STEP
    n_tiles = seq // tile
    n_total = n_batch * n_tiles
    n_blocks = n_total // TILES_PER_STEP
    blocks_per_row = seq // block
    assert seq % block == 0 and tile % (CHUNK * CHUNK_GROUP) == 0 and tile >= WINDOW
    n_keys = BAND + META_ROWS
    out_shape = (
        jax.ShapeDtypeStruct((n_batch, seq, D_MODEL), F32),
        jax.ShapeDtypeStruct((n_batch, WINDOW, KV_DIM), F32),
        jax.ShapeDtypeStruct((n_batch, WINDOW, KV_DIM), F32),
        jax.ShapeDtypeStruct((n_batch, CONV_W - 1, CONV_DIM), F32),
    )

    def x_block(offset):
        def index_map(s):
            b = jnp.clip(s + offset, 0, n_blocks - 1)
            return b // blocks_per_row, b % blocks_per_row, 0
        return index_map

    def tails_block(s):
        first_tile = jnp.clip(s * TILES_PER_STEP - 1, 0, n_total - 1)
        return first_tile // n_tiles, 0, 0

    per_batch = lambda shape: pl.BlockSpec((None,) + shape, tails_block)
    operands = (x, x, g_mix, w_in, conv_w, mk2, mv2, muc, ids, table, sinks, g_conv, g_attn, w_out,
                g_mlp, w_up, w_down, g_final)
    in_specs = [pl.BlockSpec((None, block, D_MODEL), x_block(0)), pl.BlockSpec((None, block, D_MODEL), x_block(-1))]
    in_specs += [_smem() if a is table or a is sinks else _resident(a.shape) for a in operands[2:]]
    return pl.pallas_call(
        functools.partial(_layer_kernel, tile=tile, n_tiles=n_tiles, n_total=n_total),
        grid=(n_blocks + 1,),
        in_specs=in_specs,
        out_specs=(
            pl.BlockSpec((None, block, D_MODEL), x_block(-1)),
            per_batch((WINDOW, KV_DIM)), per_batch((WINDOW, KV_DIM)), per_batch((CONV_W - 1, CONV_DIM)),
        ),
        out_shape=out_shape,
        scratch_shapes=[
            pltpu.VMEM((TILES_PER_STEP, SUBLANES + tile, CONV_DIM), F32),
            pltpu.VMEM((TILES_PER_STEP, N_KV_HEADS, WINDOW + tile, LANES), BF16),
            pltpu.VMEM((TILES_PER_STEP, N_KV_HEADS, WINDOW + tile, 2 * LANES), BF16),
            pltpu.VMEM((tile, Q_DIM), BF16),
            pltpu.VMEM((tile, Q_DIM), BF16),
            pltpu.VMEM((tile, Q_DIM), F32),
            pltpu.VMEM((WIN_CHUNKS + 1, N_KV_HEADS, GROUP * CHUNK, n_keys), F32),
            pltpu.VMEM((tile, CONV_DIM + Q_DIM), BF16),
            pltpu.VMEM((tile, D_MODEL), BF16),
            pltpu.VMEM((tile, D_MODEL), BF16),
            pltpu.VMEM((tile, IN_DIM), F32),
        ],
        compiler_params=pltpu.CompilerParams(
            dimension_semantics=("arbitrary",), vmem_limit_bytes=VMEM_LIMIT_BYTES),
        name="prompt_layer",
    )(*operands)


def _t5_bucket(rp):
    nb = N_BUCKETS // 2
    max_exact = nb // 2
    ret = jnp.where(rp > 0, nb, 0)
    n = jnp.abs(rp)
    nf = jnp.maximum(n, 1).astype(F32)
    large = max_exact + (jnp.log(nf / max_exact) / math.log(MAX_DISTANCE / max_exact)
                         * (nb - max_exact)).astype(jnp.int32)
    large = jnp.minimum(large, nb - 1)
    return ret + jnp.where(n < max_exact, n, large)


def _bucket_ids(q_pos, k_pos, valid=None):
    ids = _t5_bucket(k_pos[None, :] - q_pos[:, None])
    if valid is not None:
        ids = jnp.where(valid[None, :], ids, MASKED_ID)
    n_q = q_pos.shape[0]
    tail = jnp.concatenate([jnp.full((n_q, 1), SINK_ID, jnp.int32),
                            jnp.full((n_q, TAIL_ROWS - 1), MASKED_ID, jnp.int32)], axis=1)
    return jnp.concatenate([ids, tail], axis=1)


def _prompt_bucket_ids():
    qi = jnp.arange(CHUNK, dtype=jnp.int32)
    r = jnp.arange(BAND, dtype=jnp.int32)
    pm = jnp.arange(N_META, dtype=jnp.int32)
    out = []
    for c in range(WIN_CHUNKS + 1):
        frame_k = c * CHUNK - WIN_CHUNKS * CHUNK + r
        k_pos = jnp.concatenate([N_META + frame_k, pm])
        valid = jnp.concatenate([frame_k >= 0, jnp.ones((N_META,), bool)])
        out.append(_bucket_ids(N_META + c * CHUNK + qi, k_pos, valid))
    return jnp.stack(out)


def _sample_bucket_ids(n_win, seq):
    s = jnp.arange(seq, dtype=jnp.int32)
    k_pos = jnp.concatenate([jnp.arange(N_META, dtype=jnp.int32),
                             N_META + PAST_LEN - n_win + jnp.arange(n_win, dtype=jnp.int32),
                             N_META + PAST_LEN + s])
    return _bucket_ids(N_META + PAST_LEN + s, k_pos)


def kernel(x_prompt, x_sample, cache_k, cache_v, cache_meta_k, cache_meta_v, state_conv, meta_tokens,
           norm_mix, w_in, conv_w, attn_sinks, rel_bias_table, norm_conv_out, norm_attn_out, w_out,
           norm_mlp, w_up, w_down, norm_final):
    n_batch, seq, _ = x_prompt.shape
    s_batch, s_seq, _ = x_sample.shape
    n_win = cache_k.shape[2]
    row = lambda a: a.reshape(1, -1)

    w_in_b = w_in[0].astype(BF16)
    w_out_b = w_out[0].astype(BF16)
    w_up_b = w_up[0].astype(BF16)
    w_down_b = w_down[0].astype(BF16)
    g_mix, g_conv, g_attn = row(norm_mix[0]), row(norm_conv_out[0]), row(norm_attn_out[0])
    g_mlp, g_final = row(norm_mlp[0]), row(norm_final)
    table = rel_bias_table.astype(F32)
    sinks = attn_sinks[0].astype(F32)

    y_sample, s_k, s_v, s_conv, mk2, mv2, mk, mv, muc = _aux_call(
        x_sample.reshape(s_batch * s_seq, D_MODEL), meta_tokens, g_mix, w_in_b, conv_w[0], state_conv[0],
        cache_k[0].reshape(s_batch, n_win, KV_DIM), cache_v[0].reshape(s_batch, n_win, KV_DIM),
        cache_meta_k[0].reshape(s_batch, N_META, KV_DIM), cache_meta_v[0].reshape(s_batch, N_META, KV_DIM),
        _sample_bucket_ids(n_win, s_seq), table, sinks, g_conv, g_attn, w_out_b,
        g_mlp, w_up_b, w_down_b, g_final, n_batch=s_batch, seq=s_seq)

    y_prompt, p_k, p_v, p_conv = _layer_call(
        x_prompt, g_mix, w_in_b, conv_w[0], mk2, mv2, muc, _prompt_bucket_ids(), table, sinks,
        g_conv, g_attn, w_out_b, g_mlp, w_up_b, w_down_b, g_final)

    kv_shape = lambda a, n, length: a.reshape(1, n, length, N_KV_HEADS, HEAD_DIM)
    meta_shape = (1, n_batch, N_META, N_KV_HEADS, HEAD_DIM)
    return (
        y_prompt,
        y_sample.reshape(s_batch, s_seq, D_MODEL),
        kv_shape(p_k, n_batch, WINDOW), kv_shape(p_v, n_batch, WINDOW),
        jnp.broadcast_to(mk.reshape(1, 1, N_META, N_KV_HEADS, HEAD_DIM), meta_shape),
        jnp.broadcast_to(mv.reshape(1, 1, N_META, N_KV_HEADS, HEAD_DIM), meta_shape),
        p_conv[None],
        kv_shape(s_k, s_batch, s_seq), kv_shape(s_v, s_batch, s_seq),
        s_conv[None],
    )
```

```python
import functools
import math

import jax
import jax.numpy as jnp
from jax import lax
from jax.experimental import pallas as pl
from jax.experimental.pallas import tpu as pltpu

D_MODEL = 1024
CHUNK = 64
N_META = 16
CONV_DIM = 512
CONV_W = 3
N_HEADS = 8
N_KV_HEADS = 2
HEAD_DIM = 64
GROUP = N_HEADS // N_KV_HEADS
Q_DIM = N_HEADS * HEAD_DIM
KV_DIM = N_KV_HEADS * HEAD_DIM
IN_DIM = 3 * CONV_DIM + Q_DIM + 2 * KV_DIM
WINDOW = 128
WIN_CHUNKS = WINDOW // CHUNK
BAND = (WIN_CHUNKS + 1) * CHUNK
N_BUCKETS = 32
MAX_DISTANCE = 128
D_FF = 4 * D_MODEL
EPS = 1e-6
PAST_LEN = 4096

OFF_B, OFF_C, OFF_U = 0, CONV_DIM, 2 * CONV_DIM
OFF_Q = 3 * CONV_DIM
OFF_K = OFF_Q + Q_DIM
OFF_V = OFF_K + KV_DIM

LANES = 128
SUBLANES = 8
BF16_ROWS = 16
MXU_TILE = 256
VMEM_LIMIT_BYTES = 56 * 1024 * 1024
MASK_VALUE = float("-inf")

LAYER_TILE = 256
TILES_PER_STEP = 2
CHUNK_GROUP = 2
MLP_PIECES = 4
TAIL_ROWS = BF16_ROWS
META_ROWS = N_META + TAIL_ROWS
MASKED_ID = -1
SINK_ID = -2
HEAD_ORDER = (0, 2, 1, 3)

BF16 = jnp.bfloat16
F32 = jnp.float32


def _rms(x, g):
    ms = jnp.mean(x * x, axis=-1, keepdims=True)
    return x * lax.rsqrt(ms + EPS) * g


def _dot(a, b):
    return jnp.dot(a, b, preferred_element_type=F32)


def _dot_nt(a, b):
    return lax.dot_general(a, b, (((1,), (1,)), ((), ())), preferred_element_type=F32)


def _swap_lane_halves(x):
    pairs = [pltpu.roll(x[:, c:c + LANES], HEAD_DIM, axis=1) for c in range(0, x.shape[1], LANES)]
    return jnp.concatenate(pairs, axis=1)


def _head_layouts(kv):
    lane = lax.broadcasted_iota(jnp.int32, kv.shape, 1)
    low = lane < HEAD_DIM
    zero = jnp.zeros_like(kv)
    return (jnp.where(low, kv, zero).astype(BF16),
            jnp.where(low, pltpu.roll(kv, HEAD_DIM, axis=1), zero).astype(BF16))


def _denominator_lanes(n_rows, n_counted):
    row = lax.broadcasted_iota(jnp.int32, (n_rows, LANES), 0)
    return jnp.where(row < n_counted, 1.0, 0.0).astype(BF16)


def _build_bias(ids, table_ref, sinks_ref):
    out = []
    for h in range(N_HEADS):
        acc = jnp.full(ids.shape, MASK_VALUE, F32)
        for bucket in range(N_BUCKETS):
            acc = jnp.where(ids == bucket, table_ref[bucket, h], acc)
        out.append(jnp.where(ids == SINK_ID, sinks_ref[h], acc))
    return out


def _store_stacked_bias(bias_ref, prefix, per_head, q_rows):
    for h, b in enumerate(per_head):
        j, g = divmod(h, GROUP)
        r0 = HEAD_ORDER.index(g) * q_rows
        bias_ref[prefix + (j, slice(r0, r0 + q_rows))] = b


def _run_lengths(n_items, n_runs):
    base, extra = divmod(n_items, n_runs)
    return [base + (1 if i >= n_runs - extra else 0) for i in range(n_runs)]


def _split_columns(width, n_parts):
    size, rem = divmod(width, n_parts)
    assert rem == 0 and size % MXU_TILE == 0
    return [(i * size, (i + 1) * size) for i in range(n_parts)]


def _stack_heads(q_ref, qsw_ref, rows, j):
    c0 = j * GROUP * HEAD_DIM
    return jnp.concatenate([q_ref[rows, c0:c0 + LANES], q_ref[rows, c0 + LANES:c0 + 2 * LANES],
                            qsw_ref[rows, c0:c0 + LANES], qsw_ref[rows, c0 + LANES:c0 + 2 * LANES]], axis=0)


def _attend_scores(q_heads, key_parts, bias):
    s = _dot_nt(q_heads, jnp.concatenate(key_parts, axis=0)) + bias
    return jnp.exp(s - jnp.max(s, axis=-1, keepdims=True)).astype(BF16)


def _attend_values(probs, value_parts, q_rows):
    o = _dot(probs, jnp.concatenate(value_parts, axis=0))
    o = o[:, :LANES] / o[:, LANES:]
    even, odd = o[:2 * q_rows], pltpu.roll(o[2 * q_rows:], HEAD_DIM, axis=1)
    pairs = even + odd
    return pairs[:q_rows], pairs[q_rows:]


def _aux_kernel(xs_ref, meta_ref, g_mix_ref, w_in_ref, conv_w_ref, state_ref, ck_ref, cv_ref,
                cmk_ref, cmv_ref, ids_ref, table_ref, sinks_ref, g_conv_ref, g_attn_ref, w_out_ref,
                g_mlp_ref, w_up_hbm_ref, w_down_hbm_ref, g_final_ref,
                ys_ref, sk_ref, sv_ref, sconv_ref, mk2_ref, mv2_ref, mk_ref, mv_ref, muc_ref,
                ucb_ref, q_ref, qsw_ref, ya_ref, yc_ref, bias_ref, w_up_ref, w_down_ref, dma_sem,
                *, n_batch, seq, n_keys):
    up_copy = pltpu.make_async_copy(w_up_hbm_ref, w_up_ref, dma_sem.at[0])
    down_copy = pltpu.make_async_copy(w_down_hbm_ref, w_down_ref, dma_sem.at[1])
    up_copy.start()
    down_copy.start()

    g_mix = g_mix_ref[...]
    w_in = w_in_ref[...]
    tail = jnp.zeros((TAIL_ROWS, LANES), F32)

    mproj = _dot(_rms(meta_ref[...], g_mix).astype(BF16), w_in)
    mk = mproj[:, OFF_K:OFF_K + KV_DIM]
    mv = mproj[:, OFF_V:OFF_V + KV_DIM]
    mk_ref[...] = mk
    mv_ref[...] = mv
    muc_ref[...] = mproj[:, OFF_C:OFF_C + CONV_DIM] * mproj[:, OFF_U:OFF_U + CONV_DIM]
    for j, layout in enumerate(_head_layouts(jnp.concatenate([mk, tail], axis=0))):
        mk2_ref[j] = layout
    for j, layout in enumerate(_head_layouts(jnp.concatenate([mv, tail], axis=0))):
        mv2_ref[j] = jnp.concatenate([layout, _denominator_lanes(META_ROWS, N_META + 1)], axis=1)

    _store_stacked_bias(bias_ref, (), _build_bias(ids_ref[...], table_ref, sinks_ref), seq)

    xs = xs_ref[...]
    proj = _dot(_rms(xs, g_mix).astype(BF16), w_in)
    uc = proj[:, OFF_C:OFF_C + CONV_DIM] * proj[:, OFF_U:OFF_U + CONV_DIM]
    sk_ref[...] = proj[:, OFF_K:OFF_K + KV_DIM]
    sv_ref[...] = proj[:, OFF_V:OFF_V + KV_DIM]
    q = proj[:, OFF_Q:OFF_Q + Q_DIM] * (HEAD_DIM ** -0.5)
    q_ref[...] = q.astype(BF16)
    qsw_ref[...] = _swap_lane_halves(q).astype(BF16)
    yc_ref[...] = proj[:, OFF_B:OFF_B + CONV_DIM]
    w0 = conv_w_ref[0:1, :]
    w1 = conv_w_ref[1:2, :]
    w2 = conv_w_ref[2:3, :]
    head = SUBLANES
    for b in range(n_batch):
        ucb_ref[b, head - 2:head, :] = state_ref[b]
        ucb_ref[b, head:head + seq, :] = uc[b * seq:(b + 1) * seq]
        sconv_ref[b] = uc[(b + 1) * seq - 2:(b + 1) * seq]
    den_lanes = _denominator_lanes(n_keys + TAIL_ROWS, n_keys + 1)

    for b in range(n_batch):
        rows = slice(b * seq, (b + 1) * seq)
        conv = (w0 * ucb_ref[b, head - 2:head - 2 + seq, :] + w1 * ucb_ref[b, head - 1:head - 1 + seq, :]
                + w2 * ucb_ref[b, head:head + seq, :])
        yc_ref[rows, :] = yc_ref[rows, :] * conv
        keys = _head_layouts(jnp.concatenate([cmk_ref[b], ck_ref[b], sk_ref[rows, :], tail], axis=0))
        vals = _head_layouts(jnp.concatenate([cmv_ref[b], cv_ref[b], sv_ref[rows, :], tail], axis=0))
        for j in range(N_KV_HEADS):
            c0 = j * GROUP * HEAD_DIM
            probs = _attend_scores(_stack_heads(q_ref, qsw_ref, rows, j), [keys[j]], bias_ref[j])
            pair0, pair1 = _attend_values(probs, [jnp.concatenate([vals[j], den_lanes], axis=1)], seq)
            ya_ref[rows, c0:c0 + LANES] = pair0
            ya_ref[rows, c0 + LANES:c0 + 2 * LANES] = pair1

    yc_n = _rms(yc_ref[...], g_conv_ref[...]).astype(BF16)
    ya_n = _rms(ya_ref[...], g_attn_ref[...]).astype(BF16)
    h1 = xs + _dot(yc_n, w_out_ref[0:CONV_DIM, :]) + _dot(ya_n, w_out_ref[CONV_DIM:, :])

    up_copy.wait()
    u = jnp.maximum(_dot(_rms(h1, g_mlp_ref[...]).astype(BF16), w_up_ref[...]), 0.0)
    down_copy.wait()
    h2 = h1 + _dot((u * u).astype(BF16), w_down_ref[...])
    ys_ref[...] = _rms(h2, g_final_ref[...])


def _smem():
    return pl.BlockSpec(memory_space=pltpu.SMEM)


def _aux_call(xs, meta, g_mix, w_in, conv_w, state, ck, cv, cmk, cmv, ids, table, sinks, g_conv, g_attn, w_out,
              g_mlp, w_up, w_down, g_final, *, n_batch, seq):
    n_tok = n_batch * seq
    n_keys = ids.shape[1] - TAIL_ROWS
    out_shape = (
        jax.ShapeDtypeStruct((n_tok, D_MODEL), F32),
        jax.ShapeDtypeStruct((n_tok, KV_DIM), F32),
        jax.ShapeDtypeStruct((n_tok, KV_DIM), F32),
        jax.ShapeDtypeStruct((n_batch, CONV_W - 1, CONV_DIM), F32),
        jax.ShapeDtypeStruct((N_KV_HEADS, META_ROWS, LANES), BF16),
        jax.ShapeDtypeStruct((N_KV_HEADS, META_ROWS, 2 * LANES), BF16),
        jax.ShapeDtypeStruct((N_META, KV_DIM), F32),
        jax.ShapeDtypeStruct((N_META, KV_DIM), F32),
        jax.ShapeDtypeStruct((N_META, CONV_DIM), F32),
    )
    vmem = pl.BlockSpec(memory_space=pltpu.VMEM)
    hbm = pl.BlockSpec(memory_space=pl.ANY)
    return pl.pallas_call(
        functools.partial(_aux_kernel, n_batch=n_batch, seq=seq, n_keys=n_keys),
        in_specs=[vmem] * 11 + [_smem(), _smem()] + [vmem] * 4 + [hbm, hbm, vmem],
        out_shape=out_shape,
        scratch_shapes=[
            pltpu.VMEM((n_batch, SUBLANES + seq, CONV_DIM), F32),
            pltpu.VMEM((n_tok, Q_DIM), BF16),
            pltpu.VMEM((n_tok, Q_DIM), BF16),
            pltpu.VMEM((n_tok, Q_DIM), F32),
            pltpu.VMEM((n_tok, CONV_DIM), F32),
            pltpu.VMEM((N_KV_HEADS, GROUP * seq, n_keys + TAIL_ROWS), F32),
            pltpu.VMEM(w_up.shape, w_up.dtype),
            pltpu.VMEM(w_down.shape, w_down.dtype),
            pltpu.SemaphoreType.DMA((2,)),
        ],
        compiler_params=pltpu.CompilerParams(vmem_limit_bytes=VMEM_LIMIT_BYTES),
        name="aux_sample_layer",
    )(xs, meta, g_mix, w_in, conv_w, state, ck, cv, cmk, cmv, ids, table, sinks, g_conv, g_attn, w_out,
      g_mlp, w_up, w_down, g_final)


def _layer_kernel(xnext_ref, g_mix_ref, w_in_ref, conv_w_ref, mk2_ref, mv2_ref, muc_ref, ids_ref,
                  table_ref, sinks_ref, g_conv_ref, g_attn_ref, w_out_ref, g_mlp_ref, w_up_ref, w_down_ref,
                  g_final_ref,
                  y_ref, pk_ref, pv_ref, pc_ref,
                  ucb_ref, kb_ref, vb_ref, q_ref, qsw_ref, ya_ref, bias_ref, yn_ref, hn_ref, xn_ref, proj_ref,
                  xkeep_ref, *, tile, n_tiles, n_total):
    s = pl.program_id(0)

    @pl.when(s == 0)
    def _():
        for cls in range(WIN_CHUNKS + 1):
            _store_stacked_bias(bias_ref, (cls,), _build_bias(ids_ref[cls], table_ref, sinks_ref), CHUNK)
        ucb_ref[...] = jnp.zeros(ucb_ref.shape, F32)
        kb_ref[...] = jnp.zeros(kb_ref.shape, BF16)
        ones = _denominator_lanes(WINDOW + tile, WINDOW + tile)
        for sub in range(TILES_PER_STEP):
            for j in range(N_KV_HEADS):
                vb_ref[sub, j] = jnp.concatenate([jnp.zeros_like(ones), ones], axis=1)
        yn_ref[...] = jnp.zeros(yn_ref.shape, BF16)
        proj_ref[...] = jnp.zeros(proj_ref.shape, F32)
        xkeep_ref[...] = jnp.zeros(xkeep_ref.shape, F32)

    for sub in range(TILES_PER_STEP):
        rows = pl.ds(sub * tile, tile)
        before = (sub - 1) % TILES_PER_STEP
        _layer_tile(s * TILES_PER_STEP - 1 + sub, sub == 0,
                    xnext_ref.at[rows], xkeep_ref.at[rows], g_mix_ref, w_in_ref, conv_w_ref, mk2_ref, mv2_ref,
                    muc_ref, g_conv_ref, g_attn_ref, w_out_ref, g_mlp_ref, w_up_ref, w_down_ref, g_final_ref,
                    y_ref.at[rows], pk_ref, pv_ref, pc_ref,
                    ucb_ref.at[sub], kb_ref.at[sub], vb_ref.at[sub],
                    ucb_ref.at[before], kb_ref.at[before], vb_ref.at[before],
                    q_ref, qsw_ref, ya_ref, bias_ref, yn_ref, hn_ref, xn_ref, proj_ref,
                    tile=tile, n_tiles=n_tiles, n_total=n_total)


def _layer_tile(g, write_tails, xnext_ref, xprev_ref, g_mix_ref, w_in_ref, conv_w_ref, mk2_ref, mv2_ref,
                muc_ref, g_conv_ref, g_attn_ref, w_out_ref, g_mlp_ref, w_up_ref, w_down_ref, g_final_ref,
                y_ref, pk_ref, pv_ref, pc_ref,
                ucb_ref, kb_ref, vb_ref, ucb_before_ref, kb_before_ref, vb_before_ref,
                q_ref, qsw_ref, ya_ref, bias_ref, yn_ref, hn_ref, xn_ref, proj_ref,
                *, tile, n_tiles, n_total):
    t = lax.rem(jnp.clip(g, 0, n_total - 1), n_tiles)
    first = t == 0
    head = SUBLANES
    chunks = tile // CHUNK
    groups = chunks // CHUNK_GROUP
    pieces = _split_columns(D_FF, MLP_PIECES)
    hidden = {}

    def mlp_up(idx):
        n0, n1 = pieces[idx]
        u = jnp.maximum(_dot(hn_ref[...], w_up_ref[:, n0:n1]), 0.0)
        hidden[idx] = (u * u).astype(BF16)

    def mlp_down(acc, idx):
        n0, n1 = pieces[idx]
        return acc + _dot(hidden.pop(idx), w_down_ref[n0:n1, :])

    def scores(i):
        cls = jnp.where(first, i, WIN_CHUNKS) if i < WIN_CHUNKS else WIN_CHUNKS
        rows = slice(i * CHUNK, (i + 1) * CHUNK)
        band = slice(i * CHUNK, i * CHUNK + BAND)
        return [_attend_scores(_stack_heads(q_ref, qsw_ref, rows, j), [kb_ref[j, band, :], mk2_ref[j]],
                               bias_ref[cls, j]) for j in range(N_KV_HEADS)]

    def values(i, probs):
        rows = slice(i * CHUNK, (i + 1) * CHUNK)
        band = slice(i * CHUNK, i * CHUNK + BAND)
        for j in range(N_KV_HEADS):
            c0 = j * GROUP * HEAD_DIM
            pair0, pair1 = _attend_values(probs[j], [vb_ref[j, band, :], mv2_ref[j]], CHUNK)
            ya_ref[rows, c0:c0 + LANES] = pair0
            ya_ref[rows, c0 + LANES:c0 + 2 * LANES] = pair1

    h1 = xprev_ref[...] + _dot(yn_ref[...], w_out_ref[...])
    xprev_ref[...] = xnext_ref[...]
    hn_ref[...] = _rms(h1, g_mlp_ref[...]).astype(BF16)

    xn_ref[...] = _rms(xnext_ref[...], g_mix_ref[...]).astype(BF16)

    uc = proj_ref[:, OFF_C:OFF_C + CONV_DIM] * proj_ref[:, OFF_U:OFF_U + CONV_DIM]
    k = proj_ref[:, OFF_K:OFF_K + KV_DIM]
    v = proj_ref[:, OFF_V:OFF_V + KV_DIM]

    ucb_ref[head - 2:head, :] = jnp.where(first, muc_ref[N_META - 2:N_META, :],
                                          ucb_before_ref[head + tile - 2:head + tile, :])
    zeros = jnp.zeros((N_KV_HEADS, WINDOW, LANES), BF16)
    kb_ref[:, 0:WINDOW, :] = jnp.where(first, zeros, kb_before_ref[:, tile:tile + WINDOW, :])
    vb_ref[:, 0:WINDOW, 0:LANES] = jnp.where(first, zeros, vb_before_ref[:, tile:tile + WINDOW, 0:LANES])

    ucb_ref[head:head + tile, :] = uc
    for j, layout in enumerate(_head_layouts(k)):
        kb_ref[j, WINDOW:WINDOW + tile, :] = layout
    for j, layout in enumerate(_head_layouts(v)):
        vb_ref[j, WINDOW:WINDOW + tile, 0:LANES] = layout
    q = proj_ref[:, OFF_Q:OFF_Q + Q_DIM] * (HEAD_DIM ** -0.5)
    q_ref[...] = q.astype(BF16)
    qsw_ref[...] = _swap_lane_halves(q).astype(BF16)

    if write_tails:
        pk_ref[...] = k[tile - WINDOW:tile]
        pv_ref[...] = v[tile - WINDOW:tile]
        pc_ref[...] = uc[tile - (CONV_W - 1):tile]

    conv = (conv_w_ref[0:1, :] * ucb_ref[head - 2:head - 2 + tile, :]
            + conv_w_ref[1:2, :] * ucb_ref[head - 1:head - 1 + tile, :]
            + conv_w_ref[2:3, :] * uc)
    yc = proj_ref[:, OFF_B:OFF_B + CONV_DIM] * conv
    yn_ref[:, 0:CONV_DIM] = _rms(yc, g_conv_ref[...]).astype(BF16)

    proj_ref[:, 0:OFF_Q] = _dot(xn_ref[...], w_in_ref[:, 0:OFF_Q])

    mlp_items = [("up", 0)]
    for idx in range(1, MLP_PIECES):
        mlp_items += [("up", idx), ("down", idx - 1)]
    mlp_items.append(("down", MLP_PIECES - 1))
    runs = _run_lengths(len(mlp_items), groups + 1)

    group_of = lambda g: range(g * CHUNK_GROUP, (g + 1) * CHUNK_GROUP)
    probs = [scores(i) for i in group_of(0)]
    h2 = h1
    items = iter(mlp_items)
    for g in range(groups + 1):
        for _ in range(runs[g]):
            kind, idx = next(items)
            if kind == "up":
                mlp_up(idx)
            else:
                h2 = mlp_down(h2, idx)
        if g < groups:
            for i, p in zip(group_of(g), probs):
                values(i, p)
        if g + 1 < groups:
            probs = [scores(i) for i in group_of(g + 1)]
    y_ref[...] = _rms(h2, g_final_ref[...])
    yn_ref[:, CONV_DIM:] = _rms(ya_ref[...], g_attn_ref[...]).astype(BF16)

    proj_ref[:, OFF_Q:] = _dot(xn_ref[...], w_in_ref[:, OFF_Q:])


def _resident(shape):
    return pl.BlockSpec(shape, lambda *_: (0,) * len(shape), pipeline_mode=pl.Buffered(1))


def _layer_call(x, g_mix, w_in, conv_w, mk2, mv2, muc, ids, table, sinks, g_conv, g_attn, w_out,
                g_mlp, w_up, w_down, g_final):
    n_batch, seq, _ = x.shape
    tile = LAYER_TILE
    block = tile * TILES_PER_STEP
    n_tiles = seq // tile
    n_total = n_batch * n_tiles
    n_blocks = n_total // TILES_PER_STEP
    blocks_per_row = seq // block
    assert seq % block == 0 and tile % (CHUNK * CHUNK_GROUP) == 0 and tile >= WINDOW
    n_keys = BAND + META_ROWS
    out_shape = (
        jax.ShapeDtypeStruct((n_batch, seq, D_MODEL), F32),
        jax.ShapeDtypeStruct((n_batch, WINDOW, KV_DIM), F32),
        jax.ShapeDtypeStruct((n_batch, WINDOW, KV_DIM), F32),
        jax.ShapeDtypeStruct((n_batch, CONV_W - 1, CONV_DIM), F32),
    )

    def x_block(offset):
        def index_map(s):
            b = jnp.clip(s + offset, 0, n_blocks - 1)
            return b // blocks_per_row, b % blocks_per_row, 0
        return index_map

    def tails_block(s):
        first_tile = jnp.clip(s * TILES_PER_STEP - 1, 0, n_total - 1)
        return first_tile // n_tiles, 0, 0

    per_batch = lambda shape: pl.BlockSpec((None,) + shape, tails_block)
    operands = (x, g_mix, w_in, conv_w, mk2, mv2, muc, ids, table, sinks, g_conv, g_attn, w_out,
                g_mlp, w_up, w_down, g_final)
    in_specs = [pl.BlockSpec((None, block, D_MODEL), x_block(0))]
    in_specs += [_smem() if a is table or a is sinks else _resident(a.shape) for a in operands[1:]]
    return pl.pallas_call(
        functools.partial(_layer_kernel, tile=tile, n_tiles=n_tiles, n_total=n_total),
        grid=(n_blocks + 1,),
        in_specs=in_specs,
        out_specs=(
            pl.BlockSpec((None, block, D_MODEL), x_block(-1)),
            per_batch((WINDOW, KV_DIM)), per_batch((WINDOW, KV_DIM)), per_batch((CONV_W - 1, CONV_DIM)),
        ),
        out_shape=out_shape,
        scratch_shapes=[
            pltpu.VMEM((TILES_PER_STEP, SUBLANES + tile, CONV_DIM), F32),
            pltpu.VMEM((TILES_PER_STEP, N_KV_HEADS, WINDOW + tile, LANES), BF16),
            pltpu.VMEM((TILES_PER_STEP, N_KV_HEADS, WINDOW + tile, 2 * LANES), BF16),
            pltpu.VMEM((tile, Q_DIM), BF16),
            pltpu.VMEM((tile, Q_DIM), BF16),
            pltpu.VMEM((tile, Q_DIM), F32),
            pltpu.VMEM((WIN_CHUNKS + 1, N_KV_HEADS, GROUP * CHUNK, n_keys), F32),
            pltpu.VMEM((tile, CONV_DIM + Q_DIM), BF16),
            pltpu.VMEM((tile, D_MODEL), BF16),
            pltpu.VMEM((tile, D_MODEL), BF16),
            pltpu.VMEM((tile, IN_DIM), F32),
            pltpu.VMEM((block, D_MODEL), F32),
        ],
        compiler_params=pltpu.CompilerParams(
            dimension_semantics=("arbitrary",), vmem_limit_bytes=VMEM_LIMIT_BYTES),
        name="prompt_layer",
    )(*operands)


def _t5_bucket(rp):
    nb = N_BUCKETS // 2
    max_exact = nb // 2
    ret = jnp.where(rp > 0, nb, 0)
    n = jnp.abs(rp)
    nf = jnp.maximum(n, 1).astype(F32)
    large = max_exact + (jnp.log(nf / max_exact) / math.log(MAX_DISTANCE / max_exact)
                         * (nb - max_exact)).astype(jnp.int32)
    large = jnp.minimum(large, nb - 1)
    return ret + jnp.where(n < max_exact, n, large)


def _bucket_ids(q_pos, k_pos, valid=None):
    ids = _t5_bucket(k_pos[None, :] - q_pos[:, None])
    if valid is not None:
        ids = jnp.where(valid[None, :], ids, MASKED_ID)
    n_q = q_pos.shape[0]
    tail = jnp.concatenate([jnp.full((n_q, 1), SINK_ID, jnp.int32),
                            jnp.full((n_q, TAIL_ROWS - 1), MASKED_ID, jnp.int32)], axis=1)
    return jnp.concatenate([ids, tail], axis=1)


def _prompt_bucket_ids():
    qi = jnp.arange(CHUNK, dtype=jnp.int32)
    r = jnp.arange(BAND, dtype=jnp.int32)
    pm = jnp.arange(N_META, dtype=jnp.int32)
    out = []
    for c in range(WIN_CHUNKS + 1):
        frame_k = c * CHUNK - WIN_CHUNKS * CHUNK + r
        k_pos = jnp.concatenate([N_META + frame_k, pm])
        valid = jnp.concatenate([frame_k >= 0, jnp.ones((N_META,), bool)])
        out.append(_bucket_ids(N_META + c * CHUNK + qi, k_pos, valid))
    return jnp.stack(out)


def _sample_bucket_ids(n_win, seq):
    s = jnp.arange(seq, dtype=jnp.int32)
    k_pos = jnp.concatenate([jnp.arange(N_META, dtype=jnp.int32),
                             N_META + PAST_LEN - n_win + jnp.arange(n_win, dtype=jnp.int32),
                             N_META + PAST_LEN + s])
    return _bucket_ids(N_META + PAST_LEN + s, k_pos)


def kernel(x_prompt, x_sample, cache_k, cache_v, cache_meta_k, cache_meta_v, state_conv, meta_tokens,
           norm_mix, w_in, conv_w, attn_sinks, rel_bias_table, norm_conv_out, norm_attn_out, w_out,
           norm_mlp, w_up, w_down, norm_final):
    n_batch, seq, _ = x_prompt.shape
    s_batch, s_seq, _ = x_sample.shape
    n_win = cache_k.shape[2]
    row = lambda a: a.reshape(1, -1)

    w_in_b = w_in[0].astype(BF16)
    w_out_b = w_out[0].astype(BF16)
    w_up_b = w_up[0].astype(BF16)
    w_down_b = w_down[0].astype(BF16)
    g_mix, g_conv, g_attn = row(norm_mix[0]), row(norm_conv_out[0]), row(norm_attn_out[0])
    g_mlp, g_final = row(norm_mlp[0]), row(norm_final)
    table = rel_bias_table.astype(F32)
    sinks = attn_sinks[0].astype(F32)

    y_sample, s_k, s_v, s_conv, mk2, mv2, mk, mv, muc = _aux_call(
        x_sample.reshape(s_batch * s_seq, D_MODEL), meta_tokens, g_mix, w_in_b, conv_w[0], state_conv[0],
        cache_k[0].reshape(s_batch, n_win, KV_DIM), cache_v[0].reshape(s_batch, n_win, KV_DIM),
        cache_meta_k[0].reshape(s_batch, N_META, KV_DIM), cache_meta_v[0].reshape(s_batch, N_META, KV_DIM),
        _sample_bucket_ids(n_win, s_seq), table, sinks, g_conv, g_attn, w_out_b,
        g_mlp, w_up_b, w_down_b, g_final, n_batch=s_batch, seq=s_seq)

    y_prompt, p_k, p_v, p_conv = _layer_call(
        x_prompt, g_mix, w_in_b, conv_w[0], mk2, mv2, muc, _prompt_bucket_ids(), table, sinks,
        g_conv, g_attn, w_out_b, g_mlp, w_up_b, w_down_b, g_final)

    kv_shape = lambda a, n, length: a.reshape(1, n, length, N_KV_HEADS, HEAD_DIM)
    meta_shape = (1, n_batch, N_META, N_KV_HEADS, HEAD_DIM)
    return (
        y_prompt,
        y_sample.reshape(s_batch, s_seq, D_MODEL),
        kv_shape(p_k, n_batch, WINDOW), kv_shape(p_v, n_batch, WINDOW),
        jnp.broadcast_to(mk.reshape(1, 1, N_META, N_KV_HEADS, HEAD_DIM), meta_shape),
        jnp.broadcast_to(mv.reshape(1, 1, N_META, N_KV_HEADS, HEAD_DIM), meta_shape),
        p_conv[None],
        kv_shape(s_k, s_batch, s_seq), kv_shape(s_v, s_batch, s_seq),
        s_conv[None],
    )
```

```python
import functools
import math

import jax
import jax.numpy as jnp
from jax import lax
from jax.experimental import pallas as pl
from jax.experimental.pallas import tpu as pltpu

D_MODEL = 1024
CHUNK = 64
N_META = 16
CONV_DIM = 512
CONV_W = 3
N_HEADS = 8
N_KV_HEADS = 2
HEAD_DIM = 64
GROUP = N_HEADS // N_KV_HEADS
Q_DIM = N_HEADS * HEAD_DIM
KV_DIM = N_KV_HEADS * HEAD_DIM
IN_DIM = 3 * CONV_DIM + Q_DIM + 2 * KV_DIM
WINDOW = 128
WIN_CHUNKS = WINDOW // CHUNK
BAND = (WIN_CHUNKS + 1) * CHUNK
N_BUCKETS = 32
MAX_DISTANCE = 128
D_FF = 4 * D_MODEL
EPS = 1e-6
PAST_LEN = 4096

OFF_B, OFF_C, OFF_U = 0, CONV_DIM, 2 * CONV_DIM
OFF_Q = 3 * CONV_DIM
OFF_K = OFF_Q + Q_DIM
OFF_V = OFF_K + KV_DIM

LANES = 128
SUBLANES = 8
BF16_ROWS = 16
MXU_TILE = 256
VMEM_LIMIT_BYTES = 56 * 1024 * 1024
MASK_VALUE = float("-inf")

LAYER_TILE = 256
TILES_PER_STEP = 2
CHUNK_GROUP = 2
MLP_PIECES = 4
TAIL_ROWS = BF16_ROWS
META_ROWS = N_META + TAIL_ROWS
MASKED_ID = -1
SINK_ID = -2
HEAD_ORDER = (0, 2, 1, 3)

BF16 = jnp.bfloat16
F32 = jnp.float32


def _rms(x, g):
    ms = jnp.mean(x * x, axis=-1, keepdims=True)
    return x * lax.rsqrt(ms + EPS) * g


def _rms_unit(x):
    ms = jnp.mean(x * x, axis=-1, keepdims=True)
    return x * lax.rsqrt(ms + EPS)


def _dot(a, b):
    return jnp.dot(a, b, preferred_element_type=F32)


def _dot_nt(a, b):
    return lax.dot_general(a, b, (((1,), (1,)), ((), ())), preferred_element_type=F32)


def _swap_lane_halves(x):
    pairs = [pltpu.roll(x[:, c:c + LANES], HEAD_DIM, axis=1) for c in range(0, x.shape[1], LANES)]
    return jnp.concatenate(pairs, axis=1)


def _head_layouts(kv):
    lane = lax.broadcasted_iota(jnp.int32, kv.shape, 1)
    low = lane < HEAD_DIM
    zero = jnp.zeros_like(kv)
    return (jnp.where(low, kv, zero).astype(BF16),
            jnp.where(low, pltpu.roll(kv, HEAD_DIM, axis=1), zero).astype(BF16))


def _denominator_lanes(n_rows, n_counted):
    row = lax.broadcasted_iota(jnp.int32, (n_rows, LANES), 0)
    return jnp.where(row < n_counted, 1.0, 0.0).astype(BF16)


def _build_bias(ids, table_ref, sinks_ref):
    out = []
    for h in range(N_HEADS):
        acc = jnp.full(ids.shape, MASK_VALUE, F32)
        for bucket in range(N_BUCKETS):
            acc = jnp.where(ids == bucket, table_ref[bucket, h], acc)
        out.append(jnp.where(ids == SINK_ID, sinks_ref[h], acc))
    return out


def _store_stacked_bias(bias_ref, prefix, per_head, q_rows):
    for h, b in enumerate(per_head):
        j, g = divmod(h, GROUP)
        r0 = HEAD_ORDER.index(g) * q_rows
        bias_ref[prefix + (j, slice(r0, r0 + q_rows))] = b


def _run_lengths(n_items, n_runs):
    base, extra = divmod(n_items, n_runs)
    return [base + (1 if i >= n_runs - extra else 0) for i in range(n_runs)]


def _split_columns(width, n_parts):
    size, rem = divmod(width, n_parts)
    assert rem == 0 and size % MXU_TILE == 0
    return [(i * size, (i + 1) * size) for i in range(n_parts)]


def _stack_heads(q_ref, qsw_ref, rows, j):
    c0 = j * GROUP * HEAD_DIM
    return jnp.concatenate([q_ref[rows, c0:c0 + LANES], q_ref[rows, c0 + LANES:c0 + 2 * LANES],
                            qsw_ref[rows, c0:c0 + LANES], qsw_ref[rows, c0 + LANES:c0 + 2 * LANES]], axis=0)


def _attend_scores(q_heads, key_parts, bias):
    s = _dot_nt(q_heads, jnp.concatenate(key_parts, axis=0)) + bias
    return jnp.exp(s - jnp.max(s, axis=-1, keepdims=True)).astype(BF16)


def _attend_values(probs, value_parts, q_rows):
    o = _dot(probs, jnp.concatenate(value_parts, axis=0))
    o = o[:, :LANES] / o[:, LANES:]
    even, odd = o[:2 * q_rows], pltpu.roll(o[2 * q_rows:], HEAD_DIM, axis=1)
    pairs = even + odd
    return pairs[:q_rows], pairs[q_rows:]


def _aux_kernel(xs_ref, meta_ref, g_mix_ref, w_in_ref, conv_w_ref, state_ref, ck_ref, cv_ref,
                cmk_ref, cmv_ref, ids_ref, table_ref, sinks_ref, g_conv_ref, g_attn_ref, w_out_ref,
                g_mlp_ref, w_up_hbm_ref, w_down_hbm_ref, g_final_ref,
                ys_ref, sk_ref, sv_ref, sconv_ref, mk2_ref, mv2_ref, mk_ref, mv_ref, muc_ref,
                ucb_ref, q_ref, qsw_ref, ya_ref, yc_ref, bias_ref, w_up_ref, w_down_ref, dma_sem,
                *, n_batch, seq, n_keys):
    up_copy = pltpu.make_async_copy(w_up_hbm_ref, w_up_ref, dma_sem.at[0])
    down_copy = pltpu.make_async_copy(w_down_hbm_ref, w_down_ref, dma_sem.at[1])
    up_copy.start()
    down_copy.start()

    g_mix = g_mix_ref[...]
    w_in = w_in_ref[...]
    tail = jnp.zeros((TAIL_ROWS, LANES), F32)

    mproj = _dot(_rms_unit(meta_ref[...]).astype(BF16), w_in)
    mk = mproj[:, OFF_K:OFF_K + KV_DIM]
    mv = mproj[:, OFF_V:OFF_V + KV_DIM]
    mk_ref[...] = mk
    mv_ref[...] = mv
    muc_ref[...] = mproj[:, OFF_C:OFF_C + CONV_DIM] * mproj[:, OFF_U:OFF_U + CONV_DIM]
    for j, layout in enumerate(_head_layouts(jnp.concatenate([mk, tail], axis=0))):
        mk2_ref[j] = layout
    for j, layout in enumerate(_head_layouts(jnp.concatenate([mv, tail], axis=0))):
        mv2_ref[j] = jnp.concatenate([layout, _denominator_lanes(META_ROWS, N_META + 1)], axis=1)

    _store_stacked_bias(bias_ref, (), _build_bias(ids_ref[...], table_ref, sinks_ref), seq)

    xs = xs_ref[...]
    proj = _dot(_rms_unit(xs).astype(BF16), w_in)
    uc = proj[:, OFF_C:OFF_C + CONV_DIM] * proj[:, OFF_U:OFF_U + CONV_DIM]
    sk_ref[...] = proj[:, OFF_K:OFF_K + KV_DIM]
    sv_ref[...] = proj[:, OFF_V:OFF_V + KV_DIM]
    q = proj[:, OFF_Q:OFF_Q + Q_DIM]
    q_ref[...] = q.astype(BF16)
    qsw_ref[...] = _swap_lane_halves(q).astype(BF16)
    yc_ref[...] = proj[:, OFF_B:OFF_B + CONV_DIM]
    w0 = conv_w_ref[0:1, :]
    w1 = conv_w_ref[1:2, :]
    w2 = conv_w_ref[2:3, :]
    head = SUBLANES
    for b in range(n_batch):
        ucb_ref[b, head - 2:head, :] = state_ref[b]
        ucb_ref[b, head:head + seq, :] = uc[b * seq:(b + 1) * seq]
        sconv_ref[b] = uc[(b + 1) * seq - 2:(b + 1) * seq]
    den_lanes = _denominator_lanes(n_keys + TAIL_ROWS, n_keys + 1)

    for b in range(n_batch):
        rows = slice(b * seq, (b + 1) * seq)
        conv = (w0 * ucb_ref[b, head - 2:head - 2 + seq, :] + w1 * ucb_ref[b, head - 1:head - 1 + seq, :]
                + w2 * ucb_ref[b, head:head + seq, :])
        yc_ref[rows, :] = yc_ref[rows, :] * conv
        keys = _head_layouts(jnp.concatenate([cmk_ref[b], ck_ref[b], sk_ref[rows, :], tail], axis=0))
        vals = _head_layouts(jnp.concatenate([cmv_ref[b], cv_ref[b], sv_ref[rows, :], tail], axis=0))
        for j in range(N_KV_HEADS):
            c0 = j * GROUP * HEAD_DIM
            probs = _attend_scores(_stack_heads(q_ref, qsw_ref, rows, j), [keys[j]], bias_ref[j])
            pair0, pair1 = _attend_values(probs, [jnp.concatenate([vals[j], den_lanes], axis=1)], seq)
            ya_ref[rows, c0:c0 + LANES] = pair0
            ya_ref[rows, c0 + LANES:c0 + 2 * LANES] = pair1

    yc_n = _rms_unit(yc_ref[...]).astype(BF16)
    ya_n = _rms_unit(ya_ref[...]).astype(BF16)
    h1 = xs + _dot(yc_n, w_out_ref[0:CONV_DIM, :]) + _dot(ya_n, w_out_ref[CONV_DIM:, :])

    up_copy.wait()
    u = jnp.maximum(_dot(_rms_unit(h1).astype(BF16), w_up_ref[...]), 0.0)
    down_copy.wait()
    h2 = h1 + _dot((u * u).astype(BF16), w_down_ref[...])
    ys_ref[...] = _rms(h2, g_final_ref[...])


def _smem():
    return pl.BlockSpec(memory_space=pltpu.SMEM)


def _aux_call(xs, meta, g_mix, w_in, conv_w, state, ck, cv, cmk, cmv, ids, table, sinks, g_conv, g_attn, w_out,
              g_mlp, w_up, w_down, g_final, *, n_batch, seq):
    n_tok = n_batch * seq
    n_keys = ids.shape[1] - TAIL_ROWS
    out_shape = (
        jax.ShapeDtypeStruct((n_tok, D_MODEL), F32),
        jax.ShapeDtypeStruct((n_tok, KV_DIM), F32),
        jax.ShapeDtypeStruct((n_tok, KV_DIM), F32),
        jax.ShapeDtypeStruct((n_batch, CONV_W - 1, CONV_DIM), F32),
        jax.ShapeDtypeStruct((N_KV_HEADS, META_ROWS, LANES), BF16),
        jax.ShapeDtypeStruct((N_KV_HEADS, META_ROWS, 2 * LANES), BF16),
        jax.ShapeDtypeStruct((N_META, KV_DIM), F32),
        jax.ShapeDtypeStruct((N_META, KV_DIM), F32),
        jax.ShapeDtypeStruct((N_META, CONV_DIM), F32),
    )
    vmem = pl.BlockSpec(memory_space=pltpu.VMEM)
    hbm = pl.BlockSpec(memory_space=pl.ANY)
    return pl.pallas_call(
        functools.partial(_aux_kernel, n_batch=n_batch, seq=seq, n_keys=n_keys),
        in_specs=[vmem] * 11 + [_smem(), _smem()] + [vmem] * 4 + [hbm, hbm, vmem],
        out_shape=out_shape,
        scratch_shapes=[
            pltpu.VMEM((n_batch, SUBLANES + seq, CONV_DIM), F32),
            pltpu.VMEM((n_tok, Q_DIM), BF16),
            pltpu.VMEM((n_tok, Q_DIM), BF16),
            pltpu.VMEM((n_tok, Q_DIM), F32),
            pltpu.VMEM((n_tok, CONV_DIM), F32),
            pltpu.VMEM((N_KV_HEADS, GROUP * seq, n_keys + TAIL_ROWS), F32),
            pltpu.VMEM(w_up.shape, w_up.dtype),
            pltpu.VMEM(w_down.shape, w_down.dtype),
            pltpu.SemaphoreType.DMA((2,)),
        ],
        compiler_params=pltpu.CompilerParams(vmem_limit_bytes=VMEM_LIMIT_BYTES),
        name="aux_sample_layer",
    )(xs, meta, g_mix, w_in, conv_w, state, ck, cv, cmk, cmv, ids, table, sinks, g_conv, g_attn, w_out,
      g_mlp, w_up, w_down, g_final)


def _layer_kernel(xnext_ref, xprev_ref, g_mix_ref, w_in_ref, conv_w_ref, mk2_ref, mv2_ref, muc_ref, ids_ref,
                  table_ref, sinks_ref, g_conv_ref, g_attn_ref, w_out_ref, g_mlp_ref, w_up_ref, w_down_ref,
                  g_final_ref,
                  y_ref, pk_ref, pv_ref, pc_ref,
                  ucb_ref, kb_ref, vb_ref, q_ref, qsw_ref, ya_ref, bias_ref, yn_ref, hn_ref, xn_ref, proj_ref,
                  *, tile, n_tiles, n_total):
    s = pl.program_id(0)

    @pl.when(s == 0)
    def _():
        for cls in range(WIN_CHUNKS + 1):
            _store_stacked_bias(bias_ref, (cls,), _build_bias(ids_ref[cls], table_ref, sinks_ref), CHUNK)
        ucb_ref[...] = jnp.zeros(ucb_ref.shape, F32)
        kb_ref[...] = jnp.zeros(kb_ref.shape, BF16)
        ones = _denominator_lanes(WINDOW + tile, WINDOW + tile)
        for sub in range(TILES_PER_STEP):
            for j in range(N_KV_HEADS):
                vb_ref[sub, j] = jnp.concatenate([jnp.zeros_like(ones), ones], axis=1)
        yn_ref[...] = jnp.zeros(yn_ref.shape, BF16)
        proj_ref[...] = jnp.zeros(proj_ref.shape, F32)

    for sub in range(TILES_PER_STEP):
        rows = pl.ds(sub * tile, tile)
        before = (sub - 1) % TILES_PER_STEP
        _layer_tile(s * TILES_PER_STEP - 1 + sub, sub == 0,
                    xnext_ref.at[rows], xprev_ref.at[rows], g_mix_ref, w_in_ref, conv_w_ref, mk2_ref, mv2_ref,
                    muc_ref, g_conv_ref, g_attn_ref, w_out_ref, g_mlp_ref, w_up_ref, w_down_ref, g_final_ref,
                    y_ref.at[rows], pk_ref, pv_ref, pc_ref,
                    ucb_ref.at[sub], kb_ref.at[sub], vb_ref.at[sub],
                    ucb_ref.at[before], kb_ref.at[before], vb_ref.at[before],
                    q_ref, qsw_ref, ya_ref, bias_ref, yn_ref, hn_ref, xn_ref, proj_ref,
                    tile=tile, n_tiles=n_tiles, n_total=n_total)


def _layer_tile(g, write_tails, xnext_ref, xprev_ref, g_mix_ref, w_in_ref, conv_w_ref, mk2_ref, mv2_ref,
                muc_ref, g_conv_ref, g_attn_ref, w_out_ref, g_mlp_ref, w_up_ref, w_down_ref, g_final_ref,
                y_ref, pk_ref, pv_ref, pc_ref,
                ucb_ref, kb_ref, vb_ref, ucb_before_ref, kb_before_ref, vb_before_ref,
                q_ref, qsw_ref, ya_ref, bias_ref, yn_ref, hn_ref, xn_ref, proj_ref,
                *, tile, n_tiles, n_total):
    t = lax.rem(jnp.clip(g, 0, n_total - 1), n_tiles)
    first = t == 0
    head = SUBLANES
    chunks = tile // CHUNK
    groups = chunks // CHUNK_GROUP
    pieces = _split_columns(D_FF, MLP_PIECES)
    hidden = {}

    def mlp_up(idx):
        n0, n1 = pieces[idx]
        u = jnp.maximum(_dot(hn_ref[...], w_up_ref[:, n0:n1]), 0.0)
        hidden[idx] = (u * u).astype(BF16)

    def mlp_down(acc, idx):
        n0, n1 = pieces[idx]
        return acc + _dot(hidden.pop(idx), w_down_ref[n0:n1, :])

    def scores(i):
        cls = jnp.where(first, i, WIN_CHUNKS) if i < WIN_CHUNKS else WIN_CHUNKS
        rows = slice(i * CHUNK, (i + 1) * CHUNK)
        band = slice(i * CHUNK, i * CHUNK + BAND)
        return [_attend_scores(_stack_heads(q_ref, qsw_ref, rows, j), [kb_ref[j, band, :], mk2_ref[j]],
                               bias_ref[cls, j]) for j in range(N_KV_HEADS)]

    def values(i, probs):
        rows = slice(i * CHUNK, (i + 1) * CHUNK)
        band = slice(i * CHUNK, i * CHUNK + BAND)
        for j in range(N_KV_HEADS):
            c0 = j * GROUP * HEAD_DIM
            pair0, pair1 = _attend_values(probs[j], [vb_ref[j, band, :], mv2_ref[j]], CHUNK)
            ya_ref[rows, c0:c0 + LANES] = pair0
            ya_ref[rows, c0 + LANES:c0 + 2 * LANES] = pair1

    h1 = xprev_ref[...] + _dot(yn_ref[...], w_out_ref[...])
    hn_ref[...] = _rms_unit(h1).astype(BF16)

    xn_ref[...] = _rms_unit(xnext_ref[...]).astype(BF16)

    uc = proj_ref[:, OFF_C:OFF_C + CONV_DIM] * proj_ref[:, OFF_U:OFF_U + CONV_DIM]
    k = proj_ref[:, OFF_K:OFF_K + KV_DIM]
    v = proj_ref[:, OFF_V:OFF_V + KV_DIM]

    ucb_ref[head - 2:head, :] = jnp.where(first, muc_ref[N_META - 2:N_META, :],
                                          ucb_before_ref[head + tile - 2:head + tile, :])
    zeros = jnp.zeros((N_KV_HEADS, WINDOW, LANES), BF16)
    kb_ref[:, 0:WINDOW, :] = jnp.where(first, zeros, kb_before_ref[:, tile:tile + WINDOW, :])
    vb_ref[:, 0:WINDOW, 0:LANES] = jnp.where(first, zeros, vb_before_ref[:, tile:tile + WINDOW, 0:LANES])

    ucb_ref[head:head + tile, :] = uc
    for j, layout in enumerate(_head_layouts(k)):
        kb_ref[j, WINDOW:WINDOW + tile, :] = layout
    for j, layout in enumerate(_head_layouts(v)):
        vb_ref[j, WINDOW:WINDOW + tile, 0:LANES] = layout
    q = proj_ref[:, OFF_Q:OFF_Q + Q_DIM]
    q_ref[...] = q.astype(BF16)
    qsw_ref[...] = _swap_lane_halves(q).astype(BF16)

    if write_tails:
        pk_ref[...] = k[tile - WINDOW:tile]
        pv_ref[...] = v[tile - WINDOW:tile]
        pc_ref[...] = uc[tile - (CONV_W - 1):tile]

    conv = (conv_w_ref[0:1, :] * ucb_ref[head - 2:head - 2 + tile, :]
            + conv_w_ref[1:2, :] * ucb_ref[head - 1:head - 1 + tile, :]
            + conv_w_ref[2:3, :] * uc)
    yc = proj_ref[:, OFF_B:OFF_B + CONV_DIM] * conv
    yn_ref[:, 0:CONV_DIM] = _rms_unit(yc).astype(BF16)

    proj_ref[:, 0:OFF_Q] = _dot(xn_ref[...], w_in_ref[:, 0:OFF_Q])

    mlp_items = [("up", 0)]
    for idx in range(1, MLP_PIECES):
        mlp_items += [("up", idx), ("down", idx - 1)]
    mlp_items.append(("down", MLP_PIECES - 1))
    runs = _run_lengths(len(mlp_items), groups + 1)

    group_of = lambda g: range(g * CHUNK_GROUP, (g + 1) * CHUNK_GROUP)
    probs = [scores(i) for i in group_of(0)]
    h2 = h1
    items = iter(mlp_items)
    for g in range(groups + 1):
        for _ in range(runs[g]):
            kind, idx = next(items)
            if kind == "up":
                mlp_up(idx)
            else:
                h2 = mlp_down(h2, idx)
        if g < groups:
            for i, p in zip(group_of(g), probs):
                values(i, p)
        if g + 1 < groups:
            probs = [scores(i) for i in group_of(g + 1)]
    y_ref[...] = _rms(h2, g_final_ref[...])
    yn_ref[:, CONV_DIM:] = _rms_unit(ya_ref[...]).astype(BF16)

    proj_ref[:, OFF_Q:] = _dot(xn_ref[...], w_in_ref[:, OFF_Q:])


def _resident(shape):
    return pl.BlockSpec(shape, lambda *_: (0,) * len(shape), pipeline_mode=pl.Buffered(1))


def _layer_call(x, g_mix, w_in, conv_w, mk2, mv2, muc, ids, table, sinks, g_conv, g_attn, w_out,
                g_mlp, w_up, w_down, g_final):
    n_batch, seq, _ = x.shape
    tile = LAYER_TILE
    block = tile * TILES_PER_STEP
    n_tiles = seq // tile
    n_total = n_batch * n_tiles
    n_blocks = n_total // TILES_PER_STEP
    blocks_per_row = seq // block
    assert seq % block == 0 and tile % (CHUNK * CHUNK_GROUP) == 0 and tile >= WINDOW
    n_keys = BAND + META_ROWS
    out_shape = (
        jax.ShapeDtypeStruct((n_batch, seq, D_MODEL), F32),
        jax.ShapeDtypeStruct((n_batch, WINDOW, KV_DIM), F32),
        jax.ShapeDtypeStruct((n_batch, WINDOW, KV_DIM), F32),
        jax.ShapeDtypeStruct((n_batch, CONV_W - 1, CONV_DIM), F32),
    )

    def x_block(offset):
        def index_map(s):
            b = jnp.clip(s + offset, 0, n_blocks - 1)
            return b // blocks_per_row, b % blocks_per_row, 0
        return index_map

    def tails_block(s):
        first_tile = jnp.clip(s * TILES_PER_STEP - 1, 0, n_total - 1)
        return first_tile // n_tiles, 0, 0

    per_batch = lambda shape: pl.BlockSpec((None,) + shape, tails_block)
    operands = (x, x, g_mix, w_in, conv_w, mk2, mv2, muc, ids, table, sinks, g_conv, g_attn, w_out,
                g_mlp, w_up, w_down, g_final)
    in_specs = [pl.BlockSpec((None, block, D_MODEL), x_block(0)), pl.BlockSpec((None, block, D_MODEL), x_block(-1))]
    in_specs += [_smem() if a is table or a is sinks else _resident(a.shape) for a in operands[2:]]
    return pl.pallas_call(
        functools.partial(_layer_kernel, tile=tile, n_tiles=n_tiles, n_total=n_total),
        grid=(n_blocks + 1,),
        in_specs=in_specs,
        out_specs=(
            pl.BlockSpec((None, block, D_MODEL), x_block(-1)),
            per_batch((WINDOW, KV_DIM)), per_batch((WINDOW, KV_DIM)), per_batch((CONV_W - 1, CONV_DIM)),
        ),
        out_shape=out_shape,
        scratch_shapes=[
            pltpu.VMEM((TILES_PER_STEP, SUBLANES + tile, CONV_DIM), F32),
            pltpu.VMEM((TILES_PER_STEP, N_KV_HEADS, WINDOW + tile, LANES), BF16),
            pltpu.VMEM((TILES_PER_STEP, N_KV_HEADS, WINDOW + tile, 2 * LANES), BF16),
            pltpu.VMEM((tile, Q_DIM), BF16),
            pltpu.VMEM((tile, Q_DIM), BF16),
            pltpu.VMEM((tile, Q_DIM), F32),
            pltpu.VMEM((WIN_CHUNKS + 1, N_KV_HEADS, GROUP * CHUNK, n_keys), F32),
            pltpu.VMEM((tile, CONV_DIM + Q_DIM), BF16),
            pltpu.VMEM((tile, D_MODEL), BF16),
            pltpu.VMEM((tile, D_MODEL), BF16),
            pltpu.VMEM((tile, IN_DIM), F32),
        ],
        compiler_params=pltpu.CompilerParams(
            dimension_semantics=("arbitrary",), vmem_limit_bytes=VMEM_LIMIT_BYTES),
        name="prompt_layer",
    )(*operands)


def _t5_bucket(rp):
    nb = N_BUCKETS // 2
    max_exact = nb // 2
    ret = jnp.where(rp > 0, nb, 0)
    n = jnp.abs(rp)
    nf = jnp.maximum(n, 1).astype(F32)
    large = max_exact + (jnp.log(nf / max_exact) / math.log(MAX_DISTANCE / max_exact)
                         * (nb - max_exact)).astype(jnp.int32)
    large = jnp.minimum(large, nb - 1)
    return ret + jnp.where(n < max_exact, n, large)


def _bucket_ids(q_pos, k_pos, valid=None):
    ids = _t5_bucket(k_pos[None, :] - q_pos[:, None])
    if valid is not None:
        ids = jnp.where(valid[None, :], ids, MASKED_ID)
    n_q = q_pos.shape[0]
    tail = jnp.concatenate([jnp.full((n_q, 1), SINK_ID, jnp.int32),
                            jnp.full((n_q, TAIL_ROWS - 1), MASKED_ID, jnp.int32)], axis=1)
    return jnp.concatenate([ids, tail], axis=1)


def _prompt_bucket_ids():
    qi = jnp.arange(CHUNK, dtype=jnp.int32)
    r = jnp.arange(BAND, dtype=jnp.int32)
    pm = jnp.arange(N_META, dtype=jnp.int32)
    out = []
    for c in range(WIN_CHUNKS + 1):
        frame_k = c * CHUNK - WIN_CHUNKS * CHUNK + r
        k_pos = jnp.concatenate([N_META + frame_k, pm])
        valid = jnp.concatenate([frame_k >= 0, jnp.ones((N_META,), bool)])
        out.append(_bucket_ids(N_META + c * CHUNK + qi, k_pos, valid))
    return jnp.stack(out)


def _sample_bucket_ids(n_win, seq):
    s = jnp.arange(seq, dtype=jnp.int32)
    k_pos = jnp.concatenate([jnp.arange(N_META, dtype=jnp.int32),
                             N_META + PAST_LEN - n_win + jnp.arange(n_win, dtype=jnp.int32),
                             N_META + PAST_LEN + s])
    return _bucket_ids(N_META + PAST_LEN + s, k_pos)


def kernel(x_prompt, x_sample, cache_k, cache_v, cache_meta_k, cache_meta_v, state_conv, meta_tokens,
           norm_mix, w_in, conv_w, attn_sinks, rel_bias_table, norm_conv_out, norm_attn_out, w_out,
           norm_mlp, w_up, w_down, norm_final):
    n_batch, seq, _ = x_prompt.shape
    s_batch, s_seq, _ = x_sample.shape
    n_win = cache_k.shape[2]
    row = lambda a: a.reshape(1, -1)

    q_cols = (jnp.arange(IN_DIM) >= OFF_Q) & (jnp.arange(IN_DIM) < OFF_K)
    col_scale = jnp.where(q_cols, HEAD_DIM ** -0.5, 1.0).astype(F32)
    w_in_b = (w_in[0] * norm_mix[0][:, None] * col_scale[None, :]).astype(BF16)
    w_out_b = (w_out[0] * jnp.concatenate([norm_conv_out[0], norm_attn_out[0]])[:, None]).astype(BF16)
    w_up_b = (w_up[0] * norm_mlp[0][:, None]).astype(BF16)
    w_down_b = w_down[0].astype(BF16)
    g_mix, g_conv, g_attn = row(norm_mix[0]), row(norm_conv_out[0]), row(norm_attn_out[0])
    g_mlp, g_final = row(norm_mlp[0]), row(norm_final)
    table = rel_bias_table.astype(F32)
    sinks = attn_sinks[0].astype(F32)

    y_sample, s_k, s_v, s_conv, mk2, mv2, mk, mv, muc = _aux_call(
        x_sample.reshape(s_batch * s_seq, D_MODEL), meta_tokens, g_mix, w_in_b, conv_w[0], state_conv[0],
        cache_k[0].reshape(s_batch, n_win, KV_DIM), cache_v[0].reshape(s_batch, n_win, KV_DIM),
        cache_meta_k[0].reshape(s_batch, N_META, KV_DIM), cache_meta_v[0].reshape(s_batch, N_META, KV_DIM),
        _sample_bucket_ids(n_win, s_seq), table, sinks, g_conv, g_attn, w_out_b,
        g_mlp, w_up_b, w_down_b, g_final, n_batch=s_batch, seq=s_seq)

    y_prompt, p_k, p_v, p_conv = _layer_call(
        x_prompt, g_mix, w_in_b, conv_w[0], mk2, mv2, muc, _prompt_bucket_ids(), table, sinks,
        g_conv, g_attn, w_out_b, g_mlp, w_up_b, w_down_b, g_final)

    kv_shape = lambda a, n, length: a.reshape(1, n, length, N_KV_HEADS, HEAD_DIM)
    meta_shape = (1, n_batch, N_META, N_KV_HEADS, HEAD_DIM)
    return (
        y_prompt,
        y_sample.reshape(s_batch, s_seq, D_MODEL),
        kv_shape(p_k, n_batch, WINDOW), kv_shape(p_v, n_batch, WINDOW),
        jnp.broadcast_to(mk.reshape(1, 1, N_META, N_KV_HEADS, HEAD_DIM), meta_shape),
        jnp.broadcast_to(mv.reshape(1, 1, N_META, N_KV_HEADS, HEAD_DIM), meta_shape),
        p_conv[None],
        kv_shape(s_k, s_batch, s_seq), kv_shape(s_v, s_batch, s_seq),
        s_conv[None],
    )
```

```python
import functools
import math

import jax
import jax.numpy as jnp
from jax import lax
from jax.experimental import pallas as pl
from jax.experimental.pallas import tpu as pltpu

D_MODEL = 1024
CHUNK = 64
N_META = 16
CONV_DIM = 512
CONV_W = 3
N_HEADS = 8
N_KV_HEADS = 2
HEAD_DIM = 64
GROUP = N_HEADS // N_KV_HEADS
Q_DIM = N_HEADS * HEAD_DIM
KV_DIM = N_KV_HEADS * HEAD_DIM
IN_DIM = 3 * CONV_DIM + Q_DIM + 2 * KV_DIM
WINDOW = 128
WIN_CHUNKS = WINDOW // CHUNK
BAND = (WIN_CHUNKS + 1) * CHUNK
N_BUCKETS = 32
MAX_DISTANCE = 128
D_FF = 4 * D_MODEL
EPS = 1e-6
PAST_LEN = 4096

OFF_B, OFF_C, OFF_U = 0, CONV_DIM, 2 * CONV_DIM
OFF_Q = 3 * CONV_DIM
OFF_K = OFF_Q + Q_DIM
OFF_V = OFF_K + KV_DIM

LANES = 128
SUBLANES = 8
BF16_ROWS = 16
MXU_TILE = 256
VMEM_LIMIT_BYTES = 56 * 1024 * 1024
MASK_VALUE = float("-inf")

LAYER_TILE = 256
TILES_PER_STEP = 2
CHUNK_GROUP = 2
MLP_PIECES = 4
TAIL_ROWS = BF16_ROWS
META_ROWS = N_META + TAIL_ROWS
MASKED_ID = -1
SINK_ID = -2
HEAD_ORDER = (0, 2, 1, 3)

BF16 = jnp.bfloat16
F32 = jnp.float32


def _rms(x, g):
    ms = jnp.mean(x * x, axis=-1, keepdims=True)
    return x * lax.rsqrt(ms + EPS) * g


def _rms_unit(x):
    ms = jnp.mean(x * x, axis=-1, keepdims=True)
    return x * lax.rsqrt(ms + EPS)


def _dot(a, b):
    return jnp.dot(a, b, preferred_element_type=F32)


def _dot_nt(a, b):
    return lax.dot_general(a, b, (((1,), (1,)), ((), ())), preferred_element_type=F32)


def _swap_lane_halves(x):
    pairs = [pltpu.roll(x[:, c:c + LANES], HEAD_DIM, axis=1) for c in range(0, x.shape[1], LANES)]
    return jnp.concatenate(pairs, axis=1)


def _head_layouts(kv):
    lane = lax.broadcasted_iota(jnp.int32, kv.shape, 1)
    low = lane < HEAD_DIM
    zero = jnp.zeros_like(kv)
    return (jnp.where(low, kv, zero).astype(BF16),
            jnp.where(low, pltpu.roll(kv, HEAD_DIM, axis=1), zero).astype(BF16))


def _denominator_lanes(n_rows, n_counted):
    row = lax.broadcasted_iota(jnp.int32, (n_rows, LANES), 0)
    return jnp.where(row < n_counted, 1.0, 0.0).astype(BF16)


def _build_bias(ids, table_ref, sinks_ref):
    out = []
    for h in range(N_HEADS):
        acc = jnp.full(ids.shape, MASK_VALUE, F32)
        for bucket in range(N_BUCKETS):
            acc = jnp.where(ids == bucket, table_ref[bucket, h], acc)
        out.append(jnp.where(ids == SINK_ID, sinks_ref[h], acc))
    return out


def _store_stacked_bias(bias_ref, prefix, per_head, q_rows):
    for h, b in enumerate(per_head):
        j, g = divmod(h, GROUP)
        r0 = HEAD_ORDER.index(g) * q_rows
        bias_ref[prefix + (j, slice(r0, r0 + q_rows))] = b


def _run_lengths(n_items, n_runs):
    base, extra = divmod(n_items, n_runs)
    return [base + (1 if i >= n_runs - extra else 0) for i in range(n_runs)]


def _split_columns(width, n_parts):
    size, rem = divmod(width, n_parts)
    assert rem == 0 and size % MXU_TILE == 0
    return [(i * size, (i + 1) * size) for i in range(n_parts)]


def _stack_heads(q_ref, qsw_ref, rows, j):
    c0 = j * GROUP * HEAD_DIM
    return jnp.concatenate([q_ref[rows, c0:c0 + LANES], q_ref[rows, c0 + LANES:c0 + 2 * LANES],
                            qsw_ref[rows, c0:c0 + LANES], qsw_ref[rows, c0 + LANES:c0 + 2 * LANES]], axis=0)


def _attend_scores(q_heads, key_parts, bias):
    s = _dot_nt(q_heads, jnp.concatenate(key_parts, axis=0)) + bias
    return jnp.exp(s - jnp.max(s, axis=-1, keepdims=True)).astype(BF16)


def _attend_values(probs, value_parts, q_rows):
    o = _dot(probs, jnp.concatenate(value_parts, axis=0))
    o = o[:, :LANES] / o[:, LANES:]
    even, odd = o[:2 * q_rows], pltpu.roll(o[2 * q_rows:], HEAD_DIM, axis=1)
    pairs = even + odd
    return pairs[:q_rows], pairs[q_rows:]


def _aux_kernel(xs_ref, meta_ref, w_in_ref, conv_w_ref, state_ref, ck_ref, cv_ref,
                cmk_ref, cmv_ref, ids_ref, table_ref, sinks_ref, w_out_ref,
                w_up_hbm_ref, w_down_hbm_ref, g_final_ref,
                ys_ref, sk_ref, sv_ref, sconv_ref, mk2_ref, mv2_ref, mk_ref, mv_ref, muc_ref,
                ucb_ref, q_ref, qsw_ref, ya_ref, yc_ref, bias_ref, w_up_ref, w_down_ref, dma_sem,
                *, n_batch, seq, n_keys):
    up_copy = pltpu.make_async_copy(w_up_hbm_ref, w_up_ref, dma_sem.at[0])
    down_copy = pltpu.make_async_copy(w_down_hbm_ref, w_down_ref, dma_sem.at[1])
    up_copy.start()
    down_copy.start()

    w_in = w_in_ref[...]
    tail = jnp.zeros((TAIL_ROWS, LANES), F32)

    mproj = _dot(_rms_unit(meta_ref[...]).astype(BF16), w_in)
    mk = mproj[:, OFF_K:OFF_K + KV_DIM]
    mv = mproj[:, OFF_V:OFF_V + KV_DIM]
    mk_ref[...] = mk
    mv_ref[...] = mv
    muc_ref[...] = mproj[:, OFF_C:OFF_C + CONV_DIM] * mproj[:, OFF_U:OFF_U + CONV_DIM]
    for j, layout in enumerate(_head_layouts(jnp.concatenate([mk, tail], axis=0))):
        mk2_ref[j] = layout
    for j, layout in enumerate(_head_layouts(jnp.concatenate([mv, tail], axis=0))):
        mv2_ref[j] = jnp.concatenate([layout, _denominator_lanes(META_ROWS, N_META + 1)], axis=1)

    _store_stacked_bias(bias_ref, (), _build_bias(ids_ref[...], table_ref, sinks_ref), seq)

    xs = xs_ref[...]
    proj = _dot(_rms_unit(xs).astype(BF16), w_in)
    uc = proj[:, OFF_C:OFF_C + CONV_DIM] * proj[:, OFF_U:OFF_U + CONV_DIM]
    sk_ref[...] = proj[:, OFF_K:OFF_K + KV_DIM]
    sv_ref[...] = proj[:, OFF_V:OFF_V + KV_DIM]
    q = proj[:, OFF_Q:OFF_Q + Q_DIM]
    q_ref[...] = q.astype(BF16)
    qsw_ref[...] = _swap_lane_halves(q).astype(BF16)
    yc_ref[...] = proj[:, OFF_B:OFF_B + CONV_DIM]
    w0 = conv_w_ref[0:1, :]
    w1 = conv_w_ref[1:2, :]
    w2 = conv_w_ref[2:3, :]
    head = SUBLANES
    for b in range(n_batch):
        ucb_ref[b, head - 2:head, :] = state_ref[b]
        ucb_ref[b, head:head + seq, :] = uc[b * seq:(b + 1) * seq]
        sconv_ref[b] = uc[(b + 1) * seq - 2:(b + 1) * seq]
    den_lanes = _denominator_lanes(n_keys + TAIL_ROWS, n_keys + 1)

    for b in range(n_batch):
        rows = slice(b * seq, (b + 1) * seq)
        conv = (w0 * ucb_ref[b, head - 2:head - 2 + seq, :] + w1 * ucb_ref[b, head - 1:head - 1 + seq, :]
                + w2 * ucb_ref[b, head:head + seq, :])
        yc_ref[rows, :] = yc_ref[rows, :] * conv
        keys = _head_layouts(jnp.concatenate([cmk_ref[b], ck_ref[b], sk_ref[rows, :], tail], axis=0))
        vals = _head_layouts(jnp.concatenate([cmv_ref[b], cv_ref[b], sv_ref[rows, :], tail], axis=0))
        for j in range(N_KV_HEADS):
            c0 = j * GROUP * HEAD_DIM
            probs = _attend_scores(_stack_heads(q_ref, qsw_ref, rows, j), [keys[j]], bias_ref[j])
            pair0, pair1 = _attend_values(probs, [jnp.concatenate([vals[j], den_lanes], axis=1)], seq)
            ya_ref[rows, c0:c0 + LANES] = pair0
            ya_ref[rows, c0 + LANES:c0 + 2 * LANES] = pair1

    yc_n = _rms_unit(yc_ref[...]).astype(BF16)
    ya_n = _rms_unit(ya_ref[...]).astype(BF16)
    h1 = xs + _dot(yc_n, w_out_ref[0:CONV_DIM, :]) + _dot(ya_n, w_out_ref[CONV_DIM:, :])

    up_copy.wait()
    u = jnp.maximum(_dot(_rms_unit(h1).astype(BF16), w_up_ref[...]), 0.0)
    down_copy.wait()
    h2 = h1 + _dot((u * u).astype(BF16), w_down_ref[...])
    ys_ref[...] = _rms(h2, g_final_ref[...])


def _smem():
    return pl.BlockSpec(memory_space=pltpu.SMEM)


def _aux_call(xs, meta, w_in, conv_w, state, ck, cv, cmk, cmv, ids, table, sinks, w_out,
              w_up, w_down, g_final, *, n_batch, seq):
    n_tok = n_batch * seq
    n_keys = ids.shape[1] - TAIL_ROWS
    out_shape = (
        jax.ShapeDtypeStruct((n_tok, D_MODEL), F32),
        jax.ShapeDtypeStruct((n_tok, KV_DIM), F32),
        jax.ShapeDtypeStruct((n_tok, KV_DIM), F32),
        jax.ShapeDtypeStruct((n_batch, CONV_W - 1, CONV_DIM), F32),
        jax.ShapeDtypeStruct((N_KV_HEADS, META_ROWS, LANES), BF16),
        jax.ShapeDtypeStruct((N_KV_HEADS, META_ROWS, 2 * LANES), BF16),
        jax.ShapeDtypeStruct((N_META, KV_DIM), F32),
        jax.ShapeDtypeStruct((N_META, KV_DIM), F32),
        jax.ShapeDtypeStruct((N_META, CONV_DIM), F32),
    )
    vmem = pl.BlockSpec(memory_space=pltpu.VMEM)
    hbm = pl.BlockSpec(memory_space=pl.ANY)
    return pl.pallas_call(
        functools.partial(_aux_kernel, n_batch=n_batch, seq=seq, n_keys=n_keys),
        in_specs=[vmem] * 10 + [_smem(), _smem()] + [vmem, hbm, hbm, vmem],
        out_shape=out_shape,
        scratch_shapes=[
            pltpu.VMEM((n_batch, SUBLANES + seq, CONV_DIM), F32),
            pltpu.VMEM((n_tok, Q_DIM), BF16),
            pltpu.VMEM((n_tok, Q_DIM), BF16),
            pltpu.VMEM((n_tok, Q_DIM), F32),
            pltpu.VMEM((n_tok, CONV_DIM), F32),
            pltpu.VMEM((N_KV_HEADS, GROUP * seq, n_keys + TAIL_ROWS), F32),
            pltpu.VMEM(w_up.shape, w_up.dtype),
            pltpu.VMEM(w_down.shape, w_down.dtype),
            pltpu.SemaphoreType.DMA((2,)),
        ],
        compiler_params=pltpu.CompilerParams(vmem_limit_bytes=VMEM_LIMIT_BYTES),
        name="aux_sample_layer",
    )(xs, meta, w_in, conv_w, state, ck, cv, cmk, cmv, ids, table, sinks, w_out, w_up, w_down, g_final)


def _layer_kernel(xnext_ref, xprev_ref, w_in_ref, conv_w_ref, mk2_ref, mv2_ref, muc_ref, ids_ref,
                  table_ref, sinks_ref, w_out_ref, w_up_ref, w_down_ref,
                  g_final_ref,
                  y_ref, pk_ref, pv_ref, pc_ref,
                  ucb_ref, kb_ref, vb_ref, q_ref, qsw_ref, ya_ref, bias_ref, yn_ref, hn_ref, xn_ref, proj_ref,
                  *, tile, n_tiles, n_total):
    s = pl.program_id(0)

    @pl.when(s == 0)
    def _():
        for cls in range(WIN_CHUNKS + 1):
            _store_stacked_bias(bias_ref, (cls,), _build_bias(ids_ref[cls], table_ref, sinks_ref), CHUNK)
        ucb_ref[...] = jnp.zeros(ucb_ref.shape, F32)
        kb_ref[...] = jnp.zeros(kb_ref.shape, BF16)
        ones = _denominator_lanes(WINDOW + tile, WINDOW + tile)
        for sub in range(TILES_PER_STEP):
            for j in range(N_KV_HEADS):
                vb_ref[sub, j] = jnp.concatenate([jnp.zeros_like(ones), ones], axis=1)
        yn_ref[...] = jnp.zeros(yn_ref.shape, BF16)
        proj_ref[...] = jnp.zeros(proj_ref.shape, F32)

    for sub in range(TILES_PER_STEP):
        rows = pl.ds(sub * tile, tile)
        before = (sub - 1) % TILES_PER_STEP
        _layer_tile(s * TILES_PER_STEP - 1 + sub, sub == 0,
                    xnext_ref.at[rows], xprev_ref.at[rows], w_in_ref, conv_w_ref, mk2_ref, mv2_ref,
                    muc_ref, w_out_ref, w_up_ref, w_down_ref, g_final_ref,
                    y_ref.at[rows], pk_ref, pv_ref, pc_ref,
                    ucb_ref.at[sub], kb_ref.at[sub], vb_ref.at[sub],
                    ucb_ref.at[before], kb_ref.at[before], vb_ref.at[before],
                    q_ref, qsw_ref, ya_ref, bias_ref, yn_ref, hn_ref, xn_ref, proj_ref,
                    tile=tile, n_tiles=n_tiles, n_total=n_total)


def _layer_tile(g, write_tails, xnext_ref, xprev_ref, w_in_ref, conv_w_ref, mk2_ref, mv2_ref,
                muc_ref, w_out_ref, w_up_ref, w_down_ref, g_final_ref,
                y_ref, pk_ref, pv_ref, pc_ref,
                ucb_ref, kb_ref, vb_ref, ucb_before_ref, kb_before_ref, vb_before_ref,
                q_ref, qsw_ref, ya_ref, bias_ref, yn_ref, hn_ref, xn_ref, proj_ref,
                *, tile, n_tiles, n_total):
    t = lax.rem(jnp.clip(g, 0, n_total - 1), n_tiles)
    first = t == 0
    head = SUBLANES
    chunks = tile // CHUNK
    groups = chunks // CHUNK_GROUP
    pieces = _split_columns(D_FF, MLP_PIECES)
    hidden = {}

    def mlp_up(idx):
        n0, n1 = pieces[idx]
        u = jnp.maximum(_dot(hn_ref[...], w_up_ref[:, n0:n1]), 0.0)
        hidden[idx] = (u * u).astype(BF16)

    def mlp_down(acc, idx):
        n0, n1 = pieces[idx]
        return acc + _dot(hidden.pop(idx), w_down_ref[n0:n1, :])

    def scores(i):
        cls = jnp.where(first, i, WIN_CHUNKS) if i < WIN_CHUNKS else WIN_CHUNKS
        rows = slice(i * CHUNK, (i + 1) * CHUNK)
        band = slice(i * CHUNK, i * CHUNK + BAND)
        return [_attend_scores(_stack_heads(q_ref, qsw_ref, rows, j), [kb_ref[j, band, :], mk2_ref[j]],
                               bias_ref[cls, j]) for j in range(N_KV_HEADS)]

    def values(i, probs):
        rows = slice(i * CHUNK, (i + 1) * CHUNK)
        band = slice(i * CHUNK, i * CHUNK + BAND)
        for j in range(N_KV_HEADS):
            c0 = j * GROUP * HEAD_DIM
            pair0, pair1 = _attend_values(probs[j], [vb_ref[j, band, :], mv2_ref[j]], CHUNK)
            ya_ref[rows, c0:c0 + LANES] = pair0
            ya_ref[rows, c0 + LANES:c0 + 2 * LANES] = pair1

    h1 = xprev_ref[...] + _dot(yn_ref[...], w_out_ref[...])
    hn_ref[...] = _rms_unit(h1).astype(BF16)

    xn_ref[...] = _rms_unit(xnext_ref[...]).astype(BF16)

    uc = proj_ref[:, OFF_C:OFF_C + CONV_DIM] * proj_ref[:, OFF_U:OFF_U + CONV_DIM]
    k = proj_ref[:, OFF_K:OFF_K + KV_DIM]
    v = proj_ref[:, OFF_V:OFF_V + KV_DIM]

    ucb_ref[head - 2:head, :] = jnp.where(first, muc_ref[N_META - 2:N_META, :],
                                          ucb_before_ref[head + tile - 2:head + tile, :])
    zeros = jnp.zeros((N_KV_HEADS, WINDOW, LANES), BF16)
    kb_ref[:, 0:WINDOW, :] = jnp.where(first, zeros, kb_before_ref[:, tile:tile + WINDOW, :])
    vb_ref[:, 0:WINDOW, 0:LANES] = jnp.where(first, zeros, vb_before_ref[:, tile:tile + WINDOW, 0:LANES])

    ucb_ref[head:head + tile, :] = uc
    for j, layout in enumerate(_head_layouts(k)):
        kb_ref[j, WINDOW:WINDOW + tile, :] = layout
    for j, layout in enumerate(_head_layouts(v)):
        vb_ref[j, WINDOW:WINDOW + tile, 0:LANES] = layout
    q = proj_ref[:, OFF_Q:OFF_Q + Q_DIM]
    q_ref[...] = q.astype(BF16)
    qsw_ref[...] = _swap_lane_halves(q).astype(BF16)

    if write_tails:
        pk_ref[...] = k[tile - WINDOW:tile]
        pv_ref[...] = v[tile - WINDOW:tile]
        pc_ref[...] = uc[tile - (CONV_W - 1):tile]

    conv = (conv_w_ref[0:1, :] * ucb_ref[head - 2:head - 2 + tile, :]
            + conv_w_ref[1:2, :] * ucb_ref[head - 1:head - 1 + tile, :]
            + conv_w_ref[2:3, :] * uc)
    yc = proj_ref[:, OFF_B:OFF_B + CONV_DIM] * conv
    yn_ref[:, 0:CONV_DIM] = _rms_unit(yc).astype(BF16)

    proj_ref[:, 0:OFF_Q] = _dot(xn_ref[...], w_in_ref[:, 0:OFF_Q])

    mlp_items = [("up", 0)]
    for idx in range(1, MLP_PIECES):
        mlp_items += [("up", idx), ("down", idx - 1)]
    mlp_items.append(("down", MLP_PIECES - 1))
    runs = _run_lengths(len(mlp_items), groups + 1)

    group_of = lambda g: range(g * CHUNK_GROUP, (g + 1) * CHUNK_GROUP)
    probs = [scores(i) for i in group_of(0)]
    h2 = h1
    items = iter(mlp_items)
    for g in range(groups + 1):
        for _ in range(runs[g]):
            kind, idx = next(items)
            if kind == "up":
                mlp_up(idx)
            else:
                h2 = mlp_down(h2, idx)
        if g < groups:
            for i, p in zip(group_of(g), probs):
                values(i, p)
        if g + 1 < groups:
            probs = [scores(i) for i in group_of(g + 1)]
    y_ref[...] = _rms(h2, g_final_ref[...])
    yn_ref[:, CONV_DIM:] = _rms_unit(ya_ref[...]).astype(BF16)

    proj_ref[:, OFF_Q:] = _dot(xn_ref[...], w_in_ref[:, OFF_Q:])


def _resident(shape):
    return pl.BlockSpec(shape, lambda *_: (0,) * len(shape), pipeline_mode=pl.Buffered(1))


def _layer_call(x, w_in, conv_w, mk2, mv2, muc, ids, table, sinks, w_out, w_up, w_down, g_final):
    n_batch, seq, _ = x.shape
    tile = LAYER_TILE
    block = tile * TILES_PER_STEP
    n_tiles = seq // tile
    n_total = n_batch * n_tiles
    n_blocks = n_total // TILES_PER_STEP
    blocks_per_row = seq // block
    assert seq % block == 0 and tile % (CHUNK * CHUNK_GROUP) == 0 and tile >= WINDOW
    n_keys = BAND + META_ROWS
    out_shape = (
        jax.ShapeDtypeStruct((n_batch, seq, D_MODEL), F32),
        jax.ShapeDtypeStruct((n_batch, WINDOW, KV_DIM), F32),
        jax.ShapeDtypeStruct((n_batch, WINDOW, KV_DIM), F32),
        jax.ShapeDtypeStruct((n_batch, CONV_W - 1, CONV_DIM), F32),
    )

    def x_block(offset):
        def index_map(s):
            b = jnp.clip(s + offset, 0, n_blocks - 1)
            return b // blocks_per_row, b % blocks_per_row, 0
        return index_map

    def tails_block(s):
        first_tile = jnp.clip(s * TILES_PER_STEP - 1, 0, n_total - 1)
        return first_tile // n_tiles, 0, 0

    per_batch = lambda shape: pl.BlockSpec((None,) + shape, tails_block)
    operands = (x, x, w_in, conv_w, mk2, mv2, muc, ids, table, sinks, w_out, w_up, w_down, g_final)
    in_specs = [pl.BlockSpec((None, block, D_MODEL), x_block(0)), pl.BlockSpec((None, block, D_MODEL), x_block(-1))]
    in_specs += [_smem() if a is table or a is sinks else _resident(a.shape) for a in operands[2:]]
    return pl.pallas_call(
        functools.partial(_layer_kernel, tile=tile, n_tiles=n_tiles, n_total=n_total),
        grid=(n_blocks + 1,),
        in_specs=in_specs,
        out_specs=(
            pl.BlockSpec((None, block, D_MODEL), x_block(-1)),
            per_batch((WINDOW, KV_DIM)), per_batch((WINDOW, KV_DIM)), per_batch((CONV_W - 1, CONV_DIM)),
        ),
        out_shape=out_shape,
        scratch_shapes=[
            pltpu.VMEM((TILES_PER_STEP, SUBLANES + tile, CONV_DIM), F32),
            pltpu.VMEM((TILES_PER_STEP, N_KV_HEADS, WINDOW + tile, LANES), BF16),
            pltpu.VMEM((TILES_PER_STEP, N_KV_HEADS, WINDOW + tile, 2 * LANES), BF16),
            pltpu.VMEM((tile, Q_DIM), BF16),
            pltpu.VMEM((tile, Q_DIM), BF16),
            pltpu.VMEM((tile, Q_DIM), F32),
            pltpu.VMEM((WIN_CHUNKS + 1, N_KV_HEADS, GROUP * CHUNK, n_keys), F32),
            pltpu.VMEM((tile, CONV_DIM + Q_DIM), BF16),
            pltpu.VMEM((tile, D_MODEL), BF16),
            pltpu.VMEM((tile, D_MODEL), BF16),
            pltpu.VMEM((tile, IN_DIM), F32),
        ],
        compiler_params=pltpu.CompilerParams(
            dimension_semantics=("arbitrary",), vmem_limit_bytes=VMEM_LIMIT_BYTES),
        name="prompt_layer",
    )(*operands)


def _t5_bucket(rp):
    nb = N_BUCKETS // 2
    max_exact = nb // 2
    ret = jnp.where(rp > 0, nb, 0)
    n = jnp.abs(rp)
    nf = jnp.maximum(n, 1).astype(F32)
    large = max_exact + (jnp.log(nf / max_exact) / math.log(MAX_DISTANCE / max_exact)
                         * (nb - max_exact)).astype(jnp.int32)
    large = jnp.minimum(large, nb - 1)
    return ret + jnp.where(n < max_exact, n, large)


def _bucket_ids(q_pos, k_pos, valid=None):
    ids = _t5_bucket(k_pos[None, :] - q_pos[:, None])
    if valid is not None:
        ids = jnp.where(valid[None, :], ids, MASKED_ID)
    n_q = q_pos.shape[0]
    tail = jnp.concatenate([jnp.full((n_q, 1), SINK_ID, jnp.int32),
                            jnp.full((n_q, TAIL_ROWS - 1), MASKED_ID, jnp.int32)], axis=1)
    return jnp.concatenate([ids, tail], axis=1)


def _prompt_bucket_ids():
    qi = jnp.arange(CHUNK, dtype=jnp.int32)
    r = jnp.arange(BAND, dtype=jnp.int32)
    pm = jnp.arange(N_META, dtype=jnp.int32)
    out = []
    for c in range(WIN_CHUNKS + 1):
        frame_k = c * CHUNK - WIN_CHUNKS * CHUNK + r
        k_pos = jnp.concatenate([N_META + frame_k, pm])
        valid = jnp.concatenate([frame_k >= 0, jnp.ones((N_META,), bool)])
        out.append(_bucket_ids(N_META + c * CHUNK + qi, k_pos, valid))
    return jnp.stack(out)


def _sample_bucket_ids(n_win, seq):
    s = jnp.arange(seq, dtype=jnp.int32)
    k_pos = jnp.concatenate([jnp.arange(N_META, dtype=jnp.int32),
                             N_META + PAST_LEN - n_win + jnp.arange(n_win, dtype=jnp.int32),
                             N_META + PAST_LEN + s])
    return _bucket_ids(N_META + PAST_LEN + s, k_pos)


def kernel(x_prompt, x_sample, cache_k, cache_v, cache_meta_k, cache_meta_v, state_conv, meta_tokens,
           norm_mix, w_in, conv_w, attn_sinks, rel_bias_table, norm_conv_out, norm_attn_out, w_out,
           norm_mlp, w_up, w_down, norm_final):
    n_batch, seq, _ = x_prompt.shape
    s_batch, s_seq, _ = x_sample.shape
    n_win = cache_k.shape[2]

    q_cols = (jnp.arange(IN_DIM) >= OFF_Q) & (jnp.arange(IN_DIM) < OFF_K)
    col_scale = jnp.where(q_cols, HEAD_DIM ** -0.5, 1.0).astype(F32)
    w_in_b = (w_in[0] * norm_mix[0][:, None] * col_scale[None, :]).astype(BF16)
    w_out_b = (w_out[0] * jnp.concatenate([norm_conv_out[0], norm_attn_out[0]])[:, None]).astype(BF16)
    w_up_b = (w_up[0] * norm_mlp[0][:, None]).astype(BF16)
    w_down_b = w_down[0].astype(BF16)
    g_final = norm_final.reshape(1, -1)
    table = rel_bias_table.astype(F32)
    sinks = attn_sinks[0].astype(F32)

    y_sample, s_k, s_v, s_conv, mk2, mv2, mk, mv, muc = _aux_call(
        x_sample.reshape(s_batch * s_seq, D_MODEL), meta_tokens, w_in_b, conv_w[0], state_conv[0],
        cache_k[0].reshape(s_batch, n_win, KV_DIM), cache_v[0].reshape(s_batch, n_win, KV_DIM),
        cache_meta_k[0].reshape(s_batch, N_META, KV_DIM), cache_meta_v[0].reshape(s_batch, N_META, KV_DIM),
        _sample_bucket_ids(n_win, s_seq), table, sinks, w_out_b,
        w_up_b, w_down_b, g_final, n_batch=s_batch, seq=s_seq)

    y_prompt, p_k, p_v, p_conv = _layer_call(
        x_prompt, w_in_b, conv_w[0], mk2, mv2, muc, _prompt_bucket_ids(), table, sinks,
        w_out_b, w_up_b, w_down_b, g_final)

    kv_shape = lambda a, n, length: a.reshape(1, n, length, N_KV_HEADS, HEAD_DIM)
    meta_shape = (1, n_batch, N_META, N_KV_HEADS, HEAD_DIM)
    return (
        y_prompt,
        y_sample.reshape(s_batch, s_seq, D_MODEL),
        kv_shape(p_k, n_batch, WINDOW), kv_shape(p_v, n_batch, WINDOW),
        jnp.broadcast_to(mk.reshape(1, 1, N_META, N_KV_HEADS, HEAD_DIM), meta_shape),
        jnp.broadcast_to(mv.reshape(1, 1, N_META, N_KV_HEADS, HEAD_DIM), meta_shape),
        p_conv[None],
        kv_shape(s_k, s_batch, s_seq), kv_shape(s_v, s_batch, s_seq),
        s_conv[None],
    )
```

```python
import functools
import math

import jax
import jax.numpy as jnp
from jax import lax
from jax.experimental import pallas as pl
from jax.experimental.pallas import tpu as pltpu

D_MODEL = 1024
CHUNK = 64
N_META = 16
CONV_DIM = 512
CONV_W = 3
N_HEADS = 8
N_KV_HEADS = 2
HEAD_DIM = 64
GROUP = N_HEADS // N_KV_HEADS
Q_DIM = N_HEADS * HEAD_DIM
KV_DIM = N_KV_HEADS * HEAD_DIM
IN_DIM = 3 * CONV_DIM + Q_DIM + 2 * KV_DIM
WINDOW = 128
WIN_CHUNKS = WINDOW // CHUNK
BAND = (WIN_CHUNKS + 1) * CHUNK
N_BUCKETS = 32
MAX_DISTANCE = 128
D_FF = 4 * D_MODEL
EPS = 1e-6
PAST_LEN = 4096

OFF_B, OFF_C, OFF_U = 0, CONV_DIM, 2 * CONV_DIM
OFF_Q = 3 * CONV_DIM
OFF_K = OFF_Q + Q_DIM
OFF_V = OFF_K + KV_DIM

LANES = 128
SUBLANES = 8
BF16_ROWS = 16
MXU_TILE = 256
VMEM_LIMIT_BYTES = 56 * 1024 * 1024
MASK_VALUE = float("-inf")

LAYER_TILE = 256
TILES_PER_STEP = 2
CHUNK_GROUP = 2
MLP_PIECES = 4
TAIL_ROWS = BF16_ROWS
META_ROWS = N_META + TAIL_ROWS
MASKED_ID = -1
SINK_ID = -2
HEAD_ORDER = (0, 2, 1, 3)

BF16 = jnp.bfloat16
F32 = jnp.float32


def _rms(x, g):
    ms = jnp.mean(x * x, axis=-1, keepdims=True)
    return x * lax.rsqrt(ms + EPS) * g


def _rms_unit(x):
    ms = jnp.mean(x * x, axis=-1, keepdims=True)
    return x * lax.rsqrt(ms + EPS)


def _dot(a, b):
    return jnp.dot(a, b, preferred_element_type=F32)


def _dot_nt(a, b):
    return lax.dot_general(a, b, (((1,), (1,)), ((), ())), preferred_element_type=F32)


def _swap_lane_halves(x):
    pairs = [pltpu.roll(x[:, c:c + LANES], HEAD_DIM, axis=1) for c in range(0, x.shape[1], LANES)]
    return jnp.concatenate(pairs, axis=1)


def _head_layouts(kv):
    lane = lax.broadcasted_iota(jnp.int32, kv.shape, 1)
    low = lane < HEAD_DIM
    zero = jnp.zeros_like(kv)
    return (jnp.where(low, kv, zero).astype(BF16),
            jnp.where(low, pltpu.roll(kv, HEAD_DIM, axis=1), zero).astype(BF16))


def _denominator_lanes(n_rows, n_counted):
    row = lax.broadcasted_iota(jnp.int32, (n_rows, LANES), 0)
    return jnp.where(row < n_counted, 1.0, 0.0).astype(BF16)


def _build_bias(ids, table_ref, sinks_ref):
    out = []
    for h in range(N_HEADS):
        acc = jnp.full(ids.shape, MASK_VALUE, F32)
        for bucket in range(N_BUCKETS):
            acc = jnp.where(ids == bucket, table_ref[bucket, h], acc)
        out.append(jnp.where(ids == SINK_ID, sinks_ref[h], acc))
    return out


def _store_stacked_bias(bias_ref, prefix, per_head, q_rows):
    for h, b in enumerate(per_head):
        j, g = divmod(h, GROUP)
        r0 = HEAD_ORDER.index(g) * q_rows
        bias_ref[prefix + (j, slice(r0, r0 + q_rows))] = b


def _run_lengths(n_items, n_runs):
    base, extra = divmod(n_items, n_runs)
    return [base + (1 if i >= n_runs - extra else 0) for i in range(n_runs)]


def _split_columns(width, n_parts):
    size, rem = divmod(width, n_parts)
    assert rem == 0 and size % MXU_TILE == 0
    return [(i * size, (i + 1) * size) for i in range(n_parts)]


def _stack_heads(q_ref, qsw_ref, rows, j):
    c0 = j * GROUP * HEAD_DIM
    return jnp.concatenate([q_ref[rows, c0:c0 + LANES], q_ref[rows, c0 + LANES:c0 + 2 * LANES],
                            qsw_ref[rows, c0:c0 + LANES], qsw_ref[rows, c0 + LANES:c0 + 2 * LANES]], axis=0)


def _attend_scores(q_heads, key_parts, bias):
    s = _dot_nt(q_heads, jnp.concatenate(key_parts, axis=0)) + bias
    return jnp.exp(s - jnp.max(s, axis=-1, keepdims=True)).astype(BF16)


def _attend_values(probs, value_parts, q_rows):
    o = _dot(probs, jnp.concatenate(value_parts, axis=0))
    o = o[:, :LANES] / o[:, LANES:]
    even, odd = o[:2 * q_rows], pltpu.roll(o[2 * q_rows:], HEAD_DIM, axis=1)
    pairs = even + odd
    return pairs[:q_rows], pairs[q_rows:]


def _aux_kernel(xs_ref, meta_ref, w_in_ref, conv_w_ref, state_ref, ck_ref, cv_ref,
                cmk_ref, cmv_ref, ids_ref, table_ref, sinks_ref, w_out_ref,
                w_up_hbm_ref, w_down_hbm_ref, g_final_ref,
                ys_ref, sk_ref, sv_ref, sconv_ref, mk2_ref, mv2_ref, mk_ref, mv_ref, muc_ref,
                ucb_ref, q_ref, qsw_ref, ya_ref, yc_ref, bias_ref, w_up_ref, w_down_ref, dma_sem,
                *, n_batch, seq, n_keys):
    up_copy = pltpu.make_async_copy(w_up_hbm_ref, w_up_ref, dma_sem.at[0])
    down_copy = pltpu.make_async_copy(w_down_hbm_ref, w_down_ref, dma_sem.at[1])
    up_copy.start(priority=0)
    down_copy.start(priority=1)

    w_in = w_in_ref[...]
    tail = jnp.zeros((TAIL_ROWS, LANES), F32)

    mproj = _dot(_rms_unit(meta_ref[...]).astype(BF16), w_in)
    mk = mproj[:, OFF_K:OFF_K + KV_DIM]
    mv = mproj[:, OFF_V:OFF_V + KV_DIM]
    mk_ref[...] = mk
    mv_ref[...] = mv
    muc_ref[...] = mproj[:, OFF_C:OFF_C + CONV_DIM] * mproj[:, OFF_U:OFF_U + CONV_DIM]
    for j, layout in enumerate(_head_layouts(jnp.concatenate([mk, tail], axis=0))):
        mk2_ref[j] = layout
    for j, layout in enumerate(_head_layouts(jnp.concatenate([mv, tail], axis=0))):
        mv2_ref[j] = jnp.concatenate([layout, _denominator_lanes(META_ROWS, N_META + 1)], axis=1)

    _store_stacked_bias(bias_ref, (), _build_bias(ids_ref[...], table_ref, sinks_ref), seq)

    xs = xs_ref[...]
    proj = _dot(_rms_unit(xs).astype(BF16), w_in)
    uc = proj[:, OFF_C:OFF_C + CONV_DIM] * proj[:, OFF_U:OFF_U + CONV_DIM]
    sk_ref[...] = proj[:, OFF_K:OFF_K + KV_DIM]
    sv_ref[...] = proj[:, OFF_V:OFF_V + KV_DIM]
    q = proj[:, OFF_Q:OFF_Q + Q_DIM]
    q_ref[...] = q.astype(BF16)
    qsw_ref[...] = _swap_lane_halves(q).astype(BF16)
    yc_ref[...] = proj[:, OFF_B:OFF_B + CONV_DIM]
    w0 = conv_w_ref[0:1, :]
    w1 = conv_w_ref[1:2, :]
    w2 = conv_w_ref[2:3, :]
    head = SUBLANES
    for b in range(n_batch):
        ucb_ref[b, head - 2:head, :] = state_ref[b]
        ucb_ref[b, head:head + seq, :] = uc[b * seq:(b + 1) * seq]
        sconv_ref[b] = uc[(b + 1) * seq - 2:(b + 1) * seq]
    den_lanes = _denominator_lanes(n_keys + TAIL_ROWS, n_keys + 1)

    for b in range(n_batch):
        rows = slice(b * seq, (b + 1) * seq)
        conv = (w0 * ucb_ref[b, head - 2:head - 2 + seq, :] + w1 * ucb_ref[b, head - 1:head - 1 + seq, :]
                + w2 * ucb_ref[b, head:head + seq, :])
        yc_ref[rows, :] = yc_ref[rows, :] * conv
        keys = _head_layouts(jnp.concatenate([cmk_ref[b], ck_ref[b], sk_ref[rows, :], tail], axis=0))
        vals = _head_layouts(jnp.concatenate([cmv_ref[b], cv_ref[b], sv_ref[rows, :], tail], axis=0))
        for j in range(N_KV_HEADS):
            c0 = j * GROUP * HEAD_DIM
            probs = _attend_scores(_stack_heads(q_ref, qsw_ref, rows, j), [keys[j]], bias_ref[j])
            pair0, pair1 = _attend_values(probs, [jnp.concatenate([vals[j], den_lanes], axis=1)], seq)
            ya_ref[rows, c0:c0 + LANES] = pair0
            ya_ref[rows, c0 + LANES:c0 + 2 * LANES] = pair1

    yc_n = _rms_unit(yc_ref[...]).astype(BF16)
    ya_n = _rms_unit(ya_ref[...]).astype(BF16)
    h1 = xs + _dot(yc_n, w_out_ref[0:CONV_DIM, :]) + _dot(ya_n, w_out_ref[CONV_DIM:, :])

    up_copy.wait()
    u = jnp.maximum(_dot(_rms_unit(h1).astype(BF16), w_up_ref[...]), 0.0)
    down_copy.wait()
    h2 = h1 + _dot((u * u).astype(BF16), w_down_ref[...])
    ys_ref[...] = _rms(h2, g_final_ref[...])


def _smem():
    return pl.BlockSpec(memory_space=pltpu.SMEM)


def _aux_call(xs, meta, w_in, conv_w, state, ck, cv, cmk, cmv, ids, table, sinks, w_out,
              w_up, w_down, g_final, *, n_batch, seq):
    n_tok = n_batch * seq
    n_keys = ids.shape[1] - TAIL_ROWS
    out_shape = (
        jax.ShapeDtypeStruct((n_tok, D_MODEL), F32),
        jax.ShapeDtypeStruct((n_tok, KV_DIM), F32),
        jax.ShapeDtypeStruct((n_tok, KV_DIM), F32),
        jax.ShapeDtypeStruct((n_batch, CONV_W - 1, CONV_DIM), F32),
        jax.ShapeDtypeStruct((N_KV_HEADS, META_ROWS, LANES), BF16),
        jax.ShapeDtypeStruct((N_KV_HEADS, META_ROWS, 2 * LANES), BF16),
        jax.ShapeDtypeStruct((N_META, KV_DIM), F32),
        jax.ShapeDtypeStruct((N_META, KV_DIM), F32),
        jax.ShapeDtypeStruct((N_META, CONV_DIM), F32),
    )
    vmem = pl.BlockSpec(memory_space=pltpu.VMEM)
    hbm = pl.BlockSpec(memory_space=pl.ANY)
    return pl.pallas_call(
        functools.partial(_aux_kernel, n_batch=n_batch, seq=seq, n_keys=n_keys),
        in_specs=[vmem] * 10 + [_smem(), _smem()] + [vmem, hbm, hbm, vmem],
        out_shape=out_shape,
        scratch_shapes=[
            pltpu.VMEM((n_batch, SUBLANES + seq, CONV_DIM), F32),
            pltpu.VMEM((n_tok, Q_DIM), BF16),
            pltpu.VMEM((n_tok, Q_DIM), BF16),
            pltpu.VMEM((n_tok, Q_DIM), F32),
            pltpu.VMEM((n_tok, CONV_DIM), F32),
            pltpu.VMEM((N_KV_HEADS, GROUP * seq, n_keys + TAIL_ROWS), F32),
            pltpu.VMEM(w_up.shape, w_up.dtype),
            pltpu.VMEM(w_down.shape, w_down.dtype),
            pltpu.SemaphoreType.DMA((2,)),
        ],
        compiler_params=pltpu.CompilerParams(vmem_limit_bytes=VMEM_LIMIT_BYTES),
        name="aux_sample_layer",
    )(xs, meta, w_in, conv_w, state, ck, cv, cmk, cmv, ids, table, sinks, w_out, w_up, w_down, g_final)


def _layer_kernel(xnext_ref, xprev_ref, w_in_ref, conv_w_ref, mk2_ref, mv2_ref, muc_ref, ids_ref,
                  table_ref, sinks_ref, w_out_ref, w_up_ref, w_down_ref,
                  g_final_ref,
                  y_ref, pk_ref, pv_ref, pc_ref,
                  ucb_ref, kb_ref, vb_ref, q_ref, qsw_ref, ya_ref, bias_ref, yn_ref, hn_ref, xn_ref, proj_ref,
                  *, tile, n_tiles, n_total):
    s = pl.program_id(0)

    @pl.when(s == 0)
    def _():
        for cls in range(WIN_CHUNKS + 1):
            _store_stacked_bias(bias_ref, (cls,), _build_bias(ids_ref[cls], table_ref, sinks_ref), CHUNK)
        ucb_ref[...] = jnp.zeros(ucb_ref.shape, F32)
        kb_ref[...] = jnp.zeros(kb_ref.shape, BF16)
        ones = _denominator_lanes(WINDOW + tile, WINDOW + tile)
        for sub in range(TILES_PER_STEP):
            for j in range(N_KV_HEADS):
                vb_ref[sub, j] = jnp.concatenate([jnp.zeros_like(ones), ones], axis=1)
        yn_ref[...] = jnp.zeros(yn_ref.shape, BF16)
        proj_ref[...] = jnp.zeros(proj_ref.shape, F32)

    for sub in range(TILES_PER_STEP):
        rows = pl.ds(sub * tile, tile)
        before = (sub - 1) % TILES_PER_STEP
        _layer_tile(s * TILES_PER_STEP - 1 + sub, sub == 0,
                    xnext_ref.at[rows], xprev_ref.at[rows], w_in_ref, conv_w_ref, mk2_ref, mv2_ref,
                    muc_ref, w_out_ref, w_up_ref, w_down_ref, g_final_ref,
                    y_ref.at[rows], pk_ref, pv_ref, pc_ref,
                    ucb_ref.at[sub], kb_ref.at[sub], vb_ref.at[sub],
                    ucb_ref.at[before], kb_ref.at[before], vb_ref.at[before],
                    q_ref, qsw_ref, ya_ref, bias_ref, yn_ref, hn_ref, xn_ref, proj_ref,
                    tile=tile, n_tiles=n_tiles, n_total=n_total)


def _layer_tile(g, write_tails, xnext_ref, xprev_ref, w_in_ref, conv_w_ref, mk2_ref, mv2_ref,
                muc_ref, w_out_ref, w_up_ref, w_down_ref, g_final_ref,
                y_ref, pk_ref, pv_ref, pc_ref,
                ucb_ref, kb_ref, vb_ref, ucb_before_ref, kb_before_ref, vb_before_ref,
                q_ref, qsw_ref, ya_ref, bias_ref, yn_ref, hn_ref, xn_ref, proj_ref,
                *, tile, n_tiles, n_total):
    t = lax.rem(jnp.clip(g, 0, n_total - 1), n_tiles)
    first = t == 0
    head = SUBLANES
    chunks = tile // CHUNK
    groups = chunks // CHUNK_GROUP
    pieces = _split_columns(D_FF, MLP_PIECES)
    hidden = {}

    def mlp_up(idx):
        n0, n1 = pieces[idx]
        u = jnp.maximum(_dot(hn_ref[...], w_up_ref[:, n0:n1]), 0.0)
        hidden[idx] = (u * u).astype(BF16)

    def mlp_down(acc, idx):
        n0, n1 = pieces[idx]
        return acc + _dot(hidden.pop(idx), w_down_ref[n0:n1, :])

    def scores(i):
        cls = jnp.where(first, i, WIN_CHUNKS) if i < WIN_CHUNKS else WIN_CHUNKS
        rows = slice(i * CHUNK, (i + 1) * CHUNK)
        band = slice(i * CHUNK, i * CHUNK + BAND)
        return [_attend_scores(_stack_heads(q_ref, qsw_ref, rows, j), [kb_ref[j, band, :], mk2_ref[j]],
                               bias_ref[cls, j]) for j in range(N_KV_HEADS)]

    def values(i, probs):
        rows = slice(i * CHUNK, (i + 1) * CHUNK)
        band = slice(i * CHUNK, i * CHUNK + BAND)
        for j in range(N_KV_HEADS):
            c0 = j * GROUP * HEAD_DIM
            pair0, pair1 = _attend_values(probs[j], [vb_ref[j, band, :], mv2_ref[j]], CHUNK)
            ya_ref[rows, c0:c0 + LANES] = pair0
            ya_ref[rows, c0 + LANES:c0 + 2 * LANES] = pair1

    h1 = xprev_ref[...] + _dot(yn_ref[...], w_out_ref[...])
    hn_ref[...] = _rms_unit(h1).astype(BF16)

    xn_ref[...] = _rms_unit(xnext_ref[...]).astype(BF16)

    uc = proj_ref[:, OFF_C:OFF_C + CONV_DIM] * proj_ref[:, OFF_U:OFF_U + CONV_DIM]
    k = proj_ref[:, OFF_K:OFF_K + KV_DIM]
    v = proj_ref[:, OFF_V:OFF_V + KV_DIM]

    ucb_ref[head - 2:head, :] = jnp.where(first, muc_ref[N_META - 2:N_META, :],
                                          ucb_before_ref[head + tile - 2:head + tile, :])
    zeros = jnp.zeros((N_KV_HEADS, WINDOW, LANES), BF16)
    kb_ref[:, 0:WINDOW, :] = jnp.where(first, zeros, kb_before_ref[:, tile:tile + WINDOW, :])
    vb_ref[:, 0:WINDOW, 0:LANES] = jnp.where(first, zeros, vb_before_ref[:, tile:tile + WINDOW, 0:LANES])

    ucb_ref[head:head + tile, :] = uc
    for j, layout in enumerate(_head_layouts(k)):
        kb_ref[j, WINDOW:WINDOW + tile, :] = layout
    for j, layout in enumerate(_head_layouts(v)):
        vb_ref[j, WINDOW:WINDOW + tile, 0:LANES] = layout
    q = proj_ref[:, OFF_Q:OFF_Q + Q_DIM]
    q_ref[...] = q.astype(BF16)
    qsw_ref[...] = _swap_lane_halves(q).astype(BF16)

    if write_tails:
        pk_ref[...] = k[tile - WINDOW:tile]
        pv_ref[...] = v[tile - WINDOW:tile]
        pc_ref[...] = uc[tile - (CONV_W - 1):tile]

    conv = (conv_w_ref[0:1, :] * ucb_ref[head - 2:head - 2 + tile, :]
            + conv_w_ref[1:2, :] * ucb_ref[head - 1:head - 1 + tile, :]
            + conv_w_ref[2:3, :] * uc)
    yc = proj_ref[:, OFF_B:OFF_B + CONV_DIM] * conv
    yn_ref[:, 0:CONV_DIM] = _rms_unit(yc).astype(BF16)

    proj_ref[:, 0:OFF_Q] = _dot(xn_ref[...], w_in_ref[:, 0:OFF_Q])

    mlp_items = [("up", 0)]
    for idx in range(1, MLP_PIECES):
        mlp_items += [("up", idx), ("down", idx - 1)]
    mlp_items.append(("down", MLP_PIECES - 1))
    runs = _run_lengths(len(mlp_items), groups + 1)

    group_of = lambda g: range(g * CHUNK_GROUP, (g + 1) * CHUNK_GROUP)
    probs = [scores(i) for i in group_of(0)]
    h2 = h1
    items = iter(mlp_items)
    for g in range(groups + 1):
        for _ in range(runs[g]):
            kind, idx = next(items)
            if kind == "up":
                mlp_up(idx)
            else:
                h2 = mlp_down(h2, idx)
        if g < groups:
            for i, p in zip(group_of(g), probs):
                values(i, p)
        if g + 1 < groups:
            probs = [scores(i) for i in group_of(g + 1)]
    y_ref[...] = _rms(h2, g_final_ref[...])
    yn_ref[:, CONV_DIM:] = _rms_unit(ya_ref[...]).astype(BF16)

    proj_ref[:, OFF_Q:] = _dot(xn_ref[...], w_in_ref[:, OFF_Q:])


def _resident(shape):
    return pl.BlockSpec(shape, lambda *_: (0,) * len(shape), pipeline_mode=pl.Buffered(1))


def _layer_call(x, w_in, conv_w, mk2, mv2, muc, ids, table, sinks, w_out, w_up, w_down, g_final):
    n_batch, seq, _ = x.shape
    tile = LAYER_TILE
    block = tile * TILES_PER_STEP
    n_tiles = seq // tile
    n_total = n_batch * n_tiles
    n_blocks = n_total // TILES_PER_STEP
    blocks_per_row = seq // block
    assert seq % block == 0 and tile % (CHUNK * CHUNK_GROUP) == 0 and tile >= WINDOW
    n_keys = BAND + META_ROWS
    out_shape = (
        jax.ShapeDtypeStruct((n_batch, seq, D_MODEL), F32),
        jax.ShapeDtypeStruct((n_batch, WINDOW, KV_DIM), F32),
        jax.ShapeDtypeStruct((n_batch, WINDOW, KV_DIM), F32),
        jax.ShapeDtypeStruct((n_batch, CONV_W - 1, CONV_DIM), F32),
    )

    def x_block(offset):
        def index_map(s):
            b = jnp.clip(s + offset, 0, n_blocks - 1)
            return b // blocks_per_row, b % blocks_per_row, 0
        return index_map

    def tails_block(s):
        first_tile = jnp.clip(s * TILES_PER_STEP - 1, 0, n_total - 1)
        return first_tile // n_tiles, 0, 0

    per_batch = lambda shape: pl.BlockSpec((None,) + shape, tails_block)
    operands = (x, x, w_in, conv_w, mk2, mv2, muc, ids, table, sinks, w_out, w_up, w_down, g_final)
    in_specs = [pl.BlockSpec((None, block, D_MODEL), x_block(0)), pl.BlockSpec((None, block, D_MODEL), x_block(-1))]
    in_specs += [_smem() if a is table or a is sinks else _resident(a.shape) for a in operands[2:]]
    return pl.pallas_call(
        functools.partial(_layer_kernel, tile=tile, n_tiles=n_tiles, n_total=n_total),
        grid=(n_blocks + 1,),
        in_specs=in_specs,
        out_specs=(
            pl.BlockSpec((None, block, D_MODEL), x_block(-1)),
            per_batch((WINDOW, KV_DIM)), per_batch((WINDOW, KV_DIM)), per_batch((CONV_W - 1, CONV_DIM)),
        ),
        out_shape=out_shape,
        scratch_shapes=[
            pltpu.VMEM((TILES_PER_STEP, SUBLANES + tile, CONV_DIM), F32),
            pltpu.VMEM((TILES_PER_STEP, N_KV_HEADS, WINDOW + tile, LANES), BF16),
            pltpu.VMEM((TILES_PER_STEP, N_KV_HEADS, WINDOW + tile, 2 * LANES), BF16),
            pltpu.VMEM((tile, Q_DIM), BF16),
            pltpu.VMEM((tile, Q_DIM), BF16),
            pltpu.VMEM((tile, Q_DIM), F32),
            pltpu.VMEM((WIN_CHUNKS + 1, N_KV_HEADS, GROUP * CHUNK, n_keys), F32),
            pltpu.VMEM((tile, CONV_DIM + Q_DIM), BF16),
            pltpu.VMEM((tile, D_MODEL), BF16),
            pltpu.VMEM((tile, D_MODEL), BF16),
            pltpu.VMEM((tile, IN_DIM), F32),
        ],
        compiler_params=pltpu.CompilerParams(
            dimension_semantics=("arbitrary",), vmem_limit_bytes=VMEM_LIMIT_BYTES),
        name="prompt_layer",
    )(*operands)


def _t5_bucket(rp):
    nb = N_BUCKETS // 2
    max_exact = nb // 2
    ret = jnp.where(rp > 0, nb, 0)
    n = jnp.abs(rp)
    nf = jnp.maximum(n, 1).astype(F32)
    large = max_exact + (jnp.log(nf / max_exact) / math.log(MAX_DISTANCE / max_exact)
                         * (nb - max_exact)).astype(jnp.int32)
    large = jnp.minimum(large, nb - 1)
    return ret + jnp.where(n < max_exact, n, large)


def _bucket_ids(q_pos, k_pos, valid=None):
    ids = _t5_bucket(k_pos[None, :] - q_pos[:, None])
    if valid is not None:
        ids = jnp.where(valid[None, :], ids, MASKED_ID)
    n_q = q_pos.shape[0]
    tail = jnp.concatenate([jnp.full((n_q, 1), SINK_ID, jnp.int32),
                            jnp.full((n_q, TAIL_ROWS - 1), MASKED_ID, jnp.int32)], axis=1)
    return jnp.concatenate([ids, tail], axis=1)


def _prompt_bucket_ids():
    qi = jnp.arange(CHUNK, dtype=jnp.int32)
    r = jnp.arange(BAND, dtype=jnp.int32)
    pm = jnp.arange(N_META, dtype=jnp.int32)
    out = []
    for c in range(WIN_CHUNKS + 1):
        frame_k = c * CHUNK - WIN_CHUNKS * CHUNK + r
        k_pos = jnp.concatenate([N_META + frame_k, pm])
        valid = jnp.concatenate([frame_k >= 0, jnp.ones((N_META,), bool)])
        out.append(_bucket_ids(N_META + c * CHUNK + qi, k_pos, valid))
    return jnp.stack(out)


def _sample_bucket_ids(n_win, seq):
    s = jnp.arange(seq, dtype=jnp.int32)
    k_pos = jnp.concatenate([jnp.arange(N_META, dtype=jnp.int32),
                             N_META + PAST_LEN - n_win + jnp.arange(n_win, dtype=jnp.int32),
                             N_META + PAST_LEN + s])
    return _bucket_ids(N_META + PAST_LEN + s, k_pos)


def kernel(x_prompt, x_sample, cache_k, cache_v, cache_meta_k, cache_meta_v, state_conv, meta_tokens,
           norm_mix, w_in, conv_w, attn_sinks, rel_bias_table, norm_conv_out, norm_attn_out, w_out,
           norm_mlp, w_up, w_down, norm_final):
    n_batch, seq, _ = x_prompt.shape
    s_batch, s_seq, _ = x_sample.shape
    n_win = cache_k.shape[2]

    q_cols = (jnp.arange(IN_DIM) >= OFF_Q) & (jnp.arange(IN_DIM) < OFF_K)
    col_scale = jnp.where(q_cols, HEAD_DIM ** -0.5, 1.0).astype(F32)
    w_in_b = (w_in[0] * norm_mix[0][:, None] * col_scale[None, :]).astype(BF16)
    w_out_b = (w_out[0] * jnp.concatenate([norm_conv_out[0], norm_attn_out[0]])[:, None]).astype(BF16)
    w_up_b = (w_up[0] * norm_mlp[0][:, None]).astype(BF16)
    w_down_b = w_down[0].astype(BF16)
    g_final = norm_final.reshape(1, -1)
    table = rel_bias_table.astype(F32)
    sinks = attn_sinks[0].astype(F32)

    y_sample, s_k, s_v, s_conv, mk2, mv2, mk, mv, muc = _aux_call(
        x_sample.reshape(s_batch * s_seq, D_MODEL), meta_tokens, w_in_b, conv_w[0], state_conv[0],
        cache_k[0].reshape(s_batch, n_win, KV_DIM), cache_v[0].reshape(s_batch, n_win, KV_DIM),
        cache_meta_k[0].reshape(s_batch, N_META, KV_DIM), cache_meta_v[0].reshape(s_batch, N_META, KV_DIM),
        _sample_bucket_ids(n_win, s_seq), table, sinks, w_out_b,
        w_up_b, w_down_b, g_final, n_batch=s_batch, seq=s_seq)

    y_prompt, p_k, p_v, p_conv = _layer_call(
        x_prompt, w_in_b, conv_w[0], mk2, mv2, muc, _prompt_bucket_ids(), table, sinks,
        w_out_b, w_up_b, w_down_b, g_final)

    kv_shape = lambda a, n, length: a.reshape(1, n, length, N_KV_HEADS, HEAD_DIM)
    meta_shape = (1, n_batch, N_META, N_KV_HEADS, HEAD_DIM)
    return (
        y_prompt,
        y_sample.reshape(s_batch, s_seq, D_MODEL),
        kv_shape(p_k, n_batch, WINDOW), kv_shape(p_v, n_batch, WINDOW),
        jnp.broadcast_to(mk.reshape(1, 1, N_META, N_KV_HEADS, HEAD_DIM), meta_shape),
        jnp.broadcast_to(mv.reshape(1, 1, N_META, N_KV_HEADS, HEAD_DIM), meta_shape),
        p_conv[None],
        kv_shape(s_k, s_batch, s_seq), kv_shape(s_v, s_batch, s_seq),
        s_conv[None],
    )
```
